```python
import jax, jax.numpy as jnp
from jax import lax
import numpy as np

D_MODEL = 2048
BATCH = 2
SEQ = 4096
DEPTH = 4

PLE_DIM = 256
SC_WIDTH = D_MODEL // 2
SC_KERNEL = 3
CF_WIDTH = D_MODEL // 2
CF_KERNEL = 31
GLA_HEADS = 4
GLA_KEY_DIM = D_MODEL // 2
GLA_VAL_DIM = D_MODEL
GLA_DK = GLA_KEY_DIM // GLA_HEADS
GLA_DV = GLA_VAL_DIM // GLA_HEADS
GLA_GATE_RANK = 16
GLA_GATE_NORMALIZER = 16.0
GLA_CHUNK = 64
D_FF = -(-8 * D_MODEL // (3 * 256)) * 256
EPS = 1e-6

IN_SPLITS = (SC_WIDTH, SC_WIDTH, SC_WIDTH,
             CF_WIDTH, CF_WIDTH,
             GLA_KEY_DIM, GLA_KEY_DIM, GLA_VAL_DIM,
             GLA_VAL_DIM, GLA_GATE_RANK,
             D_MODEL, D_MODEL, D_MODEL)
IN_COLS = sum(IN_SPLITS)

kernel_name = "hybrid_conv_conformer_gla_trunk"


def rms_norm(x, g):
    xf = x.astype(jnp.float32)
    y = xf * lax.rsqrt(jnp.mean(xf * xf, axis=-1, keepdims=True) + EPS)
    return (y * g.astype(jnp.float32)).astype(x.dtype)


def layer_norm(x, g, b):
    xf = x.astype(jnp.float32)
    mu = jnp.mean(xf, axis=-1, keepdims=True)
    var = jnp.mean(jnp.square(xf - mu), axis=-1, keepdims=True)
    y = (xf - mu) * lax.rsqrt(var + EPS)
    return (y * g.astype(jnp.float32) + b.astype(jnp.float32)).astype(x.dtype)


def causal_depthwise_conv(u, w, b):
    k_width, channels = w.shape
    out = lax.conv_general_dilated(
        u, w[:, None, :].astype(u.dtype), window_strides=(1,), padding=[(k_width - 1, 0)],
        dimension_numbers=("NWC", "WIO", "NWC"), feature_group_count=channels)
    return out + b


def short_conv_mixer(gate_b, gate_c, xs, conv_w, conv_b, w_out):
    u = causal_depthwise_conv(gate_c * xs, conv_w, conv_b)
    return (gate_b * u) @ w_out


def conformer_conv_mixer(a, gate, conv_w, conv_b, ln_g, ln_b, w_out):
    u = a * jax.nn.sigmoid(gate)
    u = causal_depthwise_conv(u, conv_w, conv_b)
    u = layer_norm(u, ln_g, ln_b)
    return jax.nn.silu(u) @ w_out


def gla_chunked(q, k, v, gk):
    bsz, t_len, n_heads, dk = q.shape
    dv = v.shape[-1]
    n_chunks = t_len // GLA_CHUNK

    def to_chunks(a):
        return a.reshape(bsz, n_chunks, GLA_CHUNK, n_heads, a.shape[-1]).transpose(1, 0, 3, 2, 4)

    qc, kc, vc, gc = (to_chunks(a) for a in (q * (dk ** -0.5), k, v, gk))
    causal = jnp.tril(jnp.ones((GLA_CHUNK, GLA_CHUNK), dtype=bool))[:, :, None]

    def step(state, inp):
        qi, ki, vi, gi = inp
        b = jnp.cumsum(gi, axis=2)
        o_inter = jnp.einsum("bhcd,bhde->bhce", qi * jnp.exp(b), state)
        rel = b[:, :, :, None, :] - b[:, :, None, :, :]
        decay = jnp.exp(jnp.where(causal, rel, -jnp.inf))
        scores = jnp.einsum("bhid,bhjd,bhijd->bhij", qi, ki, decay)
        o = o_inter + jnp.einsum("bhij,bhje->bhie", scores, vi)
        b_last = b[:, :, -1:, :]
        state = (jnp.exp(b_last[:, :, 0, :])[..., None] * state
                 + jnp.einsum("bhcd,bhce->bhde", ki * jnp.exp(b_last - b), vi))
        return state, o

    s0 = jnp.zeros((bsz, n_heads, dk, dv), jnp.float32)
    _, o = lax.scan(step, s0, (qc, kc, vc, gc))
    return o.transpose(1, 0, 3, 2, 4).reshape(bsz, t_len, n_heads, dv)


def gla_mixer(q, k, v, g_out, gk_lr, w_gk, b_gk, g_norm, w_out):
    bsz, t_len, _ = q.shape
    gk = jax.nn.log_sigmoid((gk_lr @ w_gk + b_gk).astype(jnp.float32)) / GLA_GATE_NORMALIZER
    split = lambda a, d: a.astype(jnp.float32).reshape(bsz, t_len, GLA_HEADS, d)
    o = gla_chunked(split(q, GLA_DK), split(k, GLA_DK), split(v, GLA_DV), split(gk, GLA_DK))
    o = o * lax.rsqrt(jnp.mean(o * o, axis=-1, keepdims=True) + EPS) * g_norm.astype(jnp.float32)
    o = o.reshape(bsz, t_len, GLA_VAL_DIM).astype(q.dtype) * jax.nn.silu(g_out)
    return o @ w_out


def setup_inputs(seed: int = 0) -> dict:
    key = jax.random.key(seed)
    ks = jax.random.split(key, 24)
    nrm = lambda k, shape, scale: jax.random.normal(k, shape, jnp.float32) * scale
    gain = lambda k, shape: 1.0 + 0.05 * jax.random.normal(k, shape, jnp.float32)
    L = DEPTH
    return {
        "x": nrm(ks[0], (BATCH, SEQ, D_MODEL), 1.0),
        "p": nrm(ks[1], (DEPTH, BATCH, SEQ, PLE_DIM), 1.0),
        "g_mix": gain(ks[2], (L, D_MODEL)),
        "w_in": nrm(ks[3], (L, D_MODEL, IN_COLS), D_MODEL ** -0.5),
        "sc_conv_w": nrm(ks[4], (L, SC_KERNEL, SC_WIDTH), SC_KERNEL ** -0.5),
        "sc_conv_b": nrm(ks[5], (L, SC_WIDTH), 0.01),
        "w_sc_out": nrm(ks[6], (L, SC_WIDTH, D_MODEL), SC_WIDTH ** -0.5),
        "cf_conv_w": nrm(ks[7], (L, CF_KERNEL, CF_WIDTH), CF_KERNEL ** -0.5),
        "cf_conv_b": nrm(ks[8], (L, CF_WIDTH), 0.01),
        "cf_ln_g": gain(ks[9], (L, CF_WIDTH)),
        "cf_ln_b": nrm(ks[10], (L, CF_WIDTH), 0.01),
        "w_cf_out": nrm(ks[11], (L, CF_WIDTH, D_MODEL), CF_WIDTH ** -0.5),
        "w_gla_gk": nrm(ks[12], (L, GLA_GATE_RANK, GLA_KEY_DIM), GLA_GATE_RANK ** -0.5),
        "b_gla_gk": nrm(ks[13], (L, GLA_KEY_DIM), 0.01),
        "g_gla_norm": gain(ks[14], (L, GLA_DV)),
        "w_gla_out": nrm(ks[15], (L, GLA_VAL_DIM, D_MODEL), GLA_VAL_DIM ** -0.5),
        "w_o": nrm(ks[16], (L, D_MODEL, D_MODEL), D_MODEL ** -0.5),
        "g_ffn": gain(ks[17], (L, D_MODEL)),
        "w_gate_up": nrm(ks[18], (L, D_MODEL, 2 * D_FF), D_MODEL ** -0.5),
        "w_down": nrm(ks[19], (L, D_FF, D_MODEL), D_FF ** -0.5),
        "g_ple": gain(ks[20], (L, D_MODEL)),
        "w_ple_gate": nrm(ks[21], (L, D_MODEL, D_MODEL), D_MODEL ** -0.5),
        "w_ple": nrm(ks[22], (L, PLE_DIM, D_MODEL), PLE_DIM ** -0.5),
        "g_final": gain(ks[23], (D_MODEL,)),
    }


def reference(x, p, g_mix, w_in, sc_conv_w, sc_conv_b, w_sc_out, cf_conv_w, cf_conv_b,
              cf_ln_g, cf_ln_b, w_cf_out, w_gla_gk, b_gla_gk, g_gla_norm, w_gla_out, w_o,
              g_ffn, w_gate_up, w_down, g_ple, w_ple_gate, w_ple, g_final):
    split_points = np.cumsum(IN_SPLITS)[:-1].tolist()
    for i in range(DEPTH):
        h = rms_norm(x, g_mix[i])
        z = h @ w_in[i]
        (sc_b, sc_c, sc_x, cf_a, cf_g, q, k, v, g_out, gk_lr,
         m_a, m_b, m_c) = jnp.split(z, split_points, axis=-1)
        u_a = short_conv_mixer(sc_b, sc_c, sc_x, sc_conv_w[i], sc_conv_b[i], w_sc_out[i])
        u_b = conformer_conv_mixer(cf_a, cf_g, cf_conv_w[i], cf_conv_b[i],
                                   cf_ln_g[i], cf_ln_b[i], w_cf_out[i])
        u_c = gla_mixer(q, k, v, g_out, gk_lr, w_gla_gk[i], b_gla_gk[i],
                        g_gla_norm[i], w_gla_out[i])
        merged = (jax.nn.sigmoid(m_a) * u_a + jax.nn.sigmoid(m_b) * u_b
                  + jax.nn.sigmoid(m_c) * u_c)
        x = x + merged @ w_o[i]
        h = rms_norm(x, g_ffn[i])
        gate, up = jnp.split(h @ w_gate_up[i], 2, axis=-1)
        x = x + (jax.nn.silu(gate) * up) @ w_down[i]
        x = x + (p[i] @ w_ple[i]) * jax.nn.sigmoid(rms_norm(x, g_ple[i]) @ w_ple_gate[i])
    return rms_norm(x, g_final)
```

```python
import functools

import jax
import jax.numpy as jnp
from jax import lax
from jax.experimental import pallas as pl
from jax.experimental.pallas import tpu as pltpu

F32 = jnp.float32
BF16 = jnp.bfloat16

EPS = 1e-6
SC_KERNEL = 3
CF_KERNEL = 31
GLA_HEADS = 4
GLA_GATE_RANK = 16
GLA_GATE_NORMALIZER = 16.0
GLA_CHUNK = 64
GLA_SUB = 16
LANES = 128
BF16_SUBLANES = 16
VMEM_LIMIT_BYTES = 56 * 1024 * 1024


def _tile(n, want):
    t = min(want, n) // LANES * LANES
    while n % t:
        t -= LANES
    return t


def _params(*sem):
    return pltpu.CompilerParams(dimension_semantics=sem, vmem_limit_bytes=VMEM_LIMIT_BYTES)


def _rms_scale(x):
    return lax.rsqrt(jnp.mean(x * x, axis=-1, keepdims=True) + EPS)


def _sigmoid(x):
    return 1.0 / (1.0 + jnp.exp(-x))


def _dot(a, b):
    return jnp.dot(a, b, preferred_element_type=F32)


def _norm_matmul_kernel(x_ref, g_ref, w_ref, o_ref, h_scr):
    @pl.when(pl.program_id(1) == 0)
    def _():
        x = x_ref[...]
        h_scr[...] = (x * _rms_scale(x) * g_ref[...]).astype(BF16)

    o_ref[...] = _dot(h_scr[...], w_ref[...]).astype(o_ref.dtype)


def norm_matmul(x, g, w, layer, *, tm, tn, out_dtype):
    m, d = x.shape
    n = w.shape[-1]
    return pl.pallas_call(
        _norm_matmul_kernel,
        grid=(m // tm, n // tn),
        in_specs=[
            pl.BlockSpec((tm, d), lambda i, j: (i, 0)),
            pl.BlockSpec((None, 1, d), lambda i, j: (layer, 0, 0)),
            pl.BlockSpec((None, d, tn), lambda i, j: (layer, 0, j)),
        ],
        out_specs=pl.BlockSpec((tm, tn), lambda i, j: (i, j)),
        out_shape=jax.ShapeDtypeStruct((m, n), out_dtype),
        scratch_shapes=[pltpu.VMEM((tm, d), BF16)],
        compiler_params=_params("parallel", "arbitrary"),
        name="norm_matmul",
    )(x, g, w)


def _sc_kernel(b_ref, c_ref, x_ref, ch_ref, xh_ref, w_ref, bias_ref, o_ref, y_scr, *, tiles_per_seq):
    halo = ch_ref.shape[0]
    tm = c_ref.shape[0]
    first = pl.program_id(0) % tiles_per_seq == 0
    yh = ch_ref[...].astype(F32) * xh_ref[...].astype(F32)
    y_scr[0:halo, :] = jnp.where(first, 0.0, yh)
    y = c_ref[...].astype(F32) * x_ref[...].astype(F32)
    y_scr[halo:halo + tm, :] = y
    u = bias_ref[...] + w_ref[SC_KERNEL - 1:SC_KERNEL, :] * y
    for k in range(SC_KERNEL - 1):
        off = halo - (SC_KERNEL - 1) + k
        u = u + w_ref[k:k + 1, :] * y_scr[off:off + tm, :]
    o_ref[...] = (b_ref[...].astype(F32) * u).astype(o_ref.dtype)


def sc_mixer(z, conv_w, conv_b, layer, *, seq, width, col0, tm):
    m = z.shape[0]
    halo = BF16_SUBLANES
    r = tm // halo
    main = lambda c: pl.BlockSpec((tm, width), lambda i: (i, col0 + c))
    prev = lambda c: pl.BlockSpec((halo, width), lambda i: (jnp.maximum(i * r - 1, 0), col0 + c))
    return pl.pallas_call(
        functools.partial(_sc_kernel, tiles_per_seq=seq // tm),
        grid=(m // tm,),
        in_specs=[main(0), main(1), main(2), prev(1), prev(2),
                  pl.BlockSpec((None, SC_KERNEL, width), lambda i: (layer, 0, 0)),
                  pl.BlockSpec((None, 1, width), lambda i: (layer, 0, 0))],
        out_specs=pl.BlockSpec((tm, width), lambda i: (i, 0)),
        out_shape=jax.ShapeDtypeStruct((m, width), BF16),
        scratch_shapes=[pltpu.VMEM((tm + halo, width), F32)],
        compiler_params=_params("parallel"),
        name="sc_mixer",
    )(z, z, z, z, z, conv_w, conv_b)


def _cf_kernel(a_ref, g_ref, ah_ref, gh_ref, w_ref, bias_ref, lng_ref, lnb_ref, o_ref, y_scr,
               *, tiles_per_seq, rows):
    halo = ah_ref.shape[0]
    tm = a_ref.shape[0]
    first = pl.program_id(0) % tiles_per_seq == 0
    yh = ah_ref[...].astype(F32) * _sigmoid(gh_ref[...].astype(F32))
    y_scr[0:halo, :] = jnp.where(first, 0.0, yh)
    y_scr[halo:halo + tm, :] = a_ref[...].astype(F32) * _sigmoid(g_ref[...].astype(F32))
    for r0 in range(0, tm, rows):
        acc = jnp.broadcast_to(bias_ref[...], (rows, bias_ref.shape[1]))
        for k in range(CF_KERNEL):
            off = halo - (CF_KERNEL - 1) + k + r0
            acc = acc + w_ref[k:k + 1, :] * y_scr[off:off + rows, :]
        mu = jnp.mean(acc, axis=-1, keepdims=True)
        cen = acc - mu
        var = jnp.mean(cen * cen, axis=-1, keepdims=True)
        u = cen * lax.rsqrt(var + EPS) * lng_ref[...] + lnb_ref[...]
        o_ref[r0:r0 + rows, :] = (u * _sigmoid(u)).astype(o_ref.dtype)


def cf_mixer(z, conv_w, conv_b, ln_g, ln_b, layer, *, seq, width, col0, tm, rows=16):
    m = z.shape[0]
    halo = 2 * BF16_SUBLANES
    assert halo >= CF_KERNEL - 1
    r = tm // halo
    main = lambda c: pl.BlockSpec((tm, width), lambda i: (i, col0 + c))
    prev = lambda c: pl.BlockSpec((halo, width), lambda i: (jnp.maximum(i * r - 1, 0), col0 + c))
    vec = pl.BlockSpec((None, 1, width), lambda i: (layer, 0, 0))
    return pl.pallas_call(
        functools.partial(_cf_kernel, tiles_per_seq=seq // tm, rows=rows),
        grid=(m // tm,),
        in_specs=[main(0), main(1), prev(0), prev(1),
                  pl.BlockSpec((None, CF_KERNEL, width), lambda i: (layer, 0, 0)), vec, vec, vec],
        out_specs=pl.BlockSpec((tm, width), lambda i: (i, 0)),
        out_shape=jax.ShapeDtypeStruct((m, width), BF16),
        scratch_shapes=[pltpu.VMEM((tm + halo, width), F32)],
        compiler_params=_params("parallel"),
        name="cf_mixer",
    )(z, z, z, z, conv_w, conv_b, ln_g, ln_b)


def _gla_kernel(q_ref, k_ref, v_ref, go_ref, lr_ref, wgk_ref, bgk_ref, gn_ref, o_ref,
                state_scr, b_scr, k_scr, q_scr):
    tt, dk = q_ref.shape
    dv = v_ref.shape[1]
    n_sub = GLA_CHUNK // GLA_SUB

    @pl.when(pl.program_id(2) == 0)
    def _():
        state_scr[...] = jnp.zeros_like(state_scr)

    x = _dot(lr_ref[...], wgk_ref[...]) + bgk_ref[...]
    gk = (jnp.minimum(x, 0.0) - jnp.log1p(jnp.exp(-jnp.abs(x)))) * (1.0 / GLA_GATE_NORMALIZER)
    row = lax.broadcasted_iota(jnp.int32, (tt, tt), 0)
    col = lax.broadcasted_iota(jnp.int32, (tt, tt), 1)
    shift = GLA_CHUNK.bit_length() - 1
    tril = jnp.where((col <= row) & ((row >> shift) == (col >> shift)), 1.0, 0.0).astype(BF16)
    g_hi = gk.astype(BF16)
    g_lo = (gk - g_hi.astype(F32)).astype(BF16)
    b_scr[...] = _dot(tril, g_hi) + _dot(tril, g_lo)
    k_scr[...] = k_ref[...].astype(F32)
    q_scr[...] = q_ref[...].astype(F32) * (dk ** -0.5)

    rows_sub = lax.broadcasted_iota(jnp.int32, (GLA_SUB, dk), 0)
    rows_chunk = lax.broadcasted_iota(jnp.int32, (GLA_CHUNK, dk), 0)
    lane_chunk = lax.broadcasted_iota(jnp.int32, (GLA_SUB, GLA_CHUNK), 1)

    for c in range(tt // GLA_CHUNK):
        base = c * GLA_CHUNK
        bc = b_scr[base:base + GLA_CHUNK, :]
        kc = k_scr[base:base + GLA_CHUNK, :]
        qc = q_scr[base:base + GLA_CHUNK, :]
        vc = v_ref[base:base + GLA_CHUNK, :]

        score_rows = []
        for blk in range(n_sub):
            r0 = base + blk * GLA_SUB
            q_blk = q_scr[r0:r0 + GLA_SUB, :]
            b_blk = b_scr[r0:r0 + GLA_SUB, :]

            def diag_step(j, s_acc, r0=r0, q_blk=q_blk, b_blk=b_blk, blk=blk):
                bj = b_scr[pl.ds(r0 + j, 1), :]
                kj = k_scr[pl.ds(r0 + j, 1), :]
                decay = jnp.where(rows_sub >= j, jnp.exp(jnp.minimum(b_blk - bj, 0.0)), 0.0)
                s = jnp.sum(q_blk * decay * kj, axis=-1, keepdims=True)
                return jnp.where(lane_chunk == blk * GLA_SUB + j, s, s_acc)

            s_blk = lax.fori_loop(0, GLA_SUB, diag_step, jnp.zeros((GLA_SUB, GLA_CHUNK), F32))
            if blk > 0:
                b_start = b_scr[r0 - 1:r0, :]
                k_dec = jnp.where(rows_chunk < blk * GLA_SUB,
                                  jnp.exp(jnp.minimum(b_start - bc, 0.0)), 0.0)
                k_sc = (kc * k_dec).astype(BF16)
                q_sc = (q_blk * jnp.exp(jnp.minimum(b_blk - b_start, 0.0))).astype(BF16)
                s_blk = s_blk + lax.dot_general(q_sc, k_sc, (((1,), (1,)), ((), ())),
                                                preferred_element_type=F32)
            score_rows.append(s_blk)
        scores = jnp.concatenate(score_rows, axis=0).astype(BF16)

        state = state_scr[...]
        o = _dot((qc * jnp.exp(bc)).astype(BF16), state.astype(BF16)) + _dot(scores, vc)

        b_last = b_scr[base + GLA_CHUNK - 1:base + GLA_CHUNK, :]
        k_end = (kc * jnp.exp(jnp.minimum(b_last - bc, 0.0))).astype(BF16)
        upd = lax.dot_general(k_end, vc, (((0,), (0,)), ((), ())), preferred_element_type=F32)
        dec_col = jnp.transpose(jnp.broadcast_to(jnp.exp(b_last), (LANES, dk)))
        for l0 in range(0, dv, LANES):
            state_scr[:, l0:l0 + LANES] = state[:, l0:l0 + LANES] * dec_col + upd[:, l0:l0 + LANES]

        on = o * _rms_scale(o) * gn_ref[...]
        g = go_ref[base:base + GLA_CHUNK, :].astype(F32)
        o_ref[base:base + GLA_CHUNK, :] = (on * (g * _sigmoid(g))).astype(o_ref.dtype)


def gla_mixer(z, gk_lr, w_gk, b_gk, g_norm, layer, *, batch, seq, dk, dv, q_col, k_col, v_col, go_col, tt):
    m = z.shape[0]
    nt = seq // tt
    rows = lambda b, h, t: b * nt + t
    return pl.pallas_call(
        _gla_kernel,
        grid=(batch, GLA_HEADS, nt),
        in_specs=[
            pl.BlockSpec((tt, dk), lambda b, h, t: (rows(b, h, t), q_col + h)),
            pl.BlockSpec((tt, dk), lambda b, h, t: (rows(b, h, t), k_col + h)),
            pl.BlockSpec((tt, dv), lambda b, h, t: (rows(b, h, t), v_col + h)),
            pl.BlockSpec((tt, dv), lambda b, h, t: (rows(b, h, t), go_col + h)),
            pl.BlockSpec((tt, LANES), lambda b, h, t: (rows(b, h, t), 0)),
            pl.BlockSpec((None, LANES, dk), lambda b, h, t: (layer, 0, h)),
            pl.BlockSpec((None, 1, dk), lambda b, h, t: (layer, 0, h)),
            pl.BlockSpec((None, 1, dv), lambda b, h, t: (layer, 0, 0)),
        ],
        out_specs=pl.BlockSpec((tt, dv), lambda b, h, t: (rows(b, h, t), h)),
        out_shape=jax.ShapeDtypeStruct((m, GLA_HEADS * dv), BF16),
        scratch_shapes=[pltpu.VMEM((dk, dv), F32), pltpu.VMEM((tt, dk), F32),
                        pltpu.VMEM((tt, dk), F32), pltpu.VMEM((tt, dk), F32)],
        compiler_params=_params("parallel", "parallel", "arbitrary"),
        name="gla_mixer",
    )(z, z, z, z, gk_lr, w_gk, b_gk, g_norm)


def _merge_kernel(a1_ref, a2_ref, a3_ref, m1_ref, m2_ref, m3_ref, w1_ref, w2_ref, w3_ref, o_ref):
    acc = _sigmoid(m1_ref[...].astype(F32)) * _dot(a1_ref[...], w1_ref[...])
    acc = acc + _sigmoid(m2_ref[...].astype(F32)) * _dot(a2_ref[...], w2_ref[...])
    acc = acc + _sigmoid(m3_ref[...].astype(F32)) * _dot(a3_ref[...], w3_ref[...])
    o_ref[...] = acc.astype(o_ref.dtype)


def merge_mixers(a1, a2, a3, z, w1, w2, w3, layer, *, m_col, tm, tn):
    m = a1.shape[0]
    n = w1.shape[-1]
    nb = n // tn
    act = lambda a: pl.BlockSpec((tm, a.shape[1]), lambda i, j: (i, 0))
    gate = lambda c: pl.BlockSpec((tm, tn), lambda i, j: (i, m_col + c * nb + j))
    wgt = lambda w: pl.BlockSpec((None, w.shape[1], tn), lambda i, j: (layer, 0, j))
    return pl.pallas_call(
        _merge_kernel,
        grid=(m // tm, nb),
        in_specs=[act(a1), act(a2), act(a3), gate(0), gate(1), gate(2), wgt(w1), wgt(w2), wgt(w3)],
        out_specs=pl.BlockSpec((tm, tn), lambda i, j: (i, j)),
        out_shape=jax.ShapeDtypeStruct((m, n), BF16),
        compiler_params=_params("parallel", "parallel"),
        name="merge_mixers",
    )(a1, a2, a3, z, z, z, w1, w2, w3)


def _matmul_residual_kernel(a_ref, w_ref, x_ref, o_ref):
    o_ref[...] = x_ref[...] + _dot(a_ref[...], w_ref[...])


def matmul_residual(a, w, x, layer, *, tm, tn):
    m, k = a.shape
    n = w.shape[-1]
    return pl.pallas_call(
        _matmul_residual_kernel,
        grid=(m // tm, n // tn),
        in_specs=[pl.BlockSpec((tm, k), lambda i, j: (i, 0)),
                  pl.BlockSpec((None, k, tn), lambda i, j: (layer, 0, j)),
                  pl.BlockSpec((tm, tn), lambda i, j: (i, j))],
        out_specs=pl.BlockSpec((tm, tn), lambda i, j: (i, j)),
        out_shape=jax.ShapeDtypeStruct((m, n), F32),
        compiler_params=_params("parallel", "parallel"),
        name="matmul_residual",
    )(a, w, x)


def _gate_up_kernel(x_ref, g_ref, wg_ref, wu_ref, o_ref, h_scr):
    @pl.when(pl.program_id(1) == 0)
    def _():
        x = x_ref[...]
        h_scr[...] = (x * _rms_scale(x) * g_ref[...]).astype(BF16)

    h = h_scr[...]
    gate = _dot(h, wg_ref[...])
    up = _dot(h, wu_ref[...])
    o_ref[...] = (gate * _sigmoid(gate) * up).astype(o_ref.dtype)


def gate_up(x, g, w, layer, *, tm, tn):
    m, d = x.shape
    f = w.shape[-1] // 2
    nb = f // tn
    return pl.pallas_call(
        _gate_up_kernel,
        grid=(m // tm, nb),
        in_specs=[pl.BlockSpec((tm, d), lambda i, j: (i, 0)),
                  pl.BlockSpec((None, 1, d), lambda i, j: (layer, 0, 0)),
                  pl.BlockSpec((None, d, tn), lambda i, j: (layer, 0, j)),
                  pl.BlockSpec((None, d, tn), lambda i, j: (layer, 0, nb + j))],
        out_specs=pl.BlockSpec((tm, tn), lambda i, j: (i, j)),
        out_shape=jax.ShapeDtypeStruct((m, f), BF16),
        scratch_shapes=[pltpu.VMEM((tm, d), BF16)],
        compiler_params=_params("parallel", "arbitrary"),
        name="gate_up",
    )(x, g, w, w)


def _ple_kernel(x_ref, g_ref, p_ref, wp_ref, wg_ref, o_ref, h_scr, p_scr):
    tn = o_ref.shape[1]

    @pl.when(pl.program_id(1) == 0)
    def _():
        x = x_ref[...]
        h_scr[...] = (x * _rms_scale(x) * g_ref[...]).astype(BF16)
        p_scr[...] = p_ref[...].astype(BF16)

    j0 = pl.multiple_of(pl.program_id(1) * tn, tn)
    emb = _dot(p_scr[...], wp_ref[...])
    gate = _sigmoid(_dot(h_scr[...], wg_ref[...]))
    o_ref[...] = x_ref[:, pl.ds(j0, tn)] + emb * gate


def ple(x, g, p, wp, wg, layer, *, tm, tn):
    m, d = x.shape
    e = p.shape[-1]
    return pl.pallas_call(
        _ple_kernel,
        grid=(m // tm, d // tn),
        in_specs=[pl.BlockSpec((tm, d), lambda i, j: (i, 0)),
                  pl.BlockSpec((None, 1, d), lambda i, j: (layer, 0, 0)),
                  pl.BlockSpec((None, tm, e), lambda i, j: (layer, i, 0)),
                  pl.BlockSpec((None, e, tn), lambda i, j: (layer, 0, j)),
                  pl.BlockSpec((None, d, tn), lambda i, j: (layer, 0, j))],
        out_specs=pl.BlockSpec((tm, tn), lambda i, j: (i, j)),
        out_shape=jax.ShapeDtypeStruct((m, d), F32),
        scratch_shapes=[pltpu.VMEM((tm, d), BF16), pltpu.VMEM((tm, e), BF16)],
        compiler_params=_params("parallel", "arbitrary"),
        name="ple",
    )(x, g, p, wp, wg)


def _rmsnorm_kernel(x_ref, g_ref, o_ref):
    x = x_ref[...]
    o_ref[...] = x * _rms_scale(x) * g_ref[...]


def rmsnorm(x, g, *, tm):
    m, d = x.shape
    return pl.pallas_call(
        _rmsnorm_kernel,
        grid=(m // tm,),
        in_specs=[pl.BlockSpec((tm, d), lambda i: (i, 0)), pl.BlockSpec((1, d), lambda i: (0, 0))],
        out_specs=pl.BlockSpec((tm, d), lambda i: (i, 0)),
        out_shape=jax.ShapeDtypeStruct((m, d), F32),
        compiler_params=_params("parallel"),
        name="final_rmsnorm",
    )(x, g)


def kernel(x, p, g_mix, w_in, sc_conv_w, sc_conv_b, w_sc_out, cf_conv_w, cf_conv_b, cf_ln_g, cf_ln_b,
           w_cf_out, w_gla_gk, b_gla_gk, g_gla_norm, w_gla_out, w_o, g_ffn, w_gate_up, w_down, g_ple,
           w_ple_gate, w_ple, g_final):
    batch, seq, d = x.shape
    depth = p.shape[0]
    m = batch * seq
    sc_w = sc_conv_w.shape[-1]
    cf_w = cf_conv_w.shape[-1]
    key_dim = w_gla_gk.shape[-1]
    val_dim = w_gla_out.shape[1]
    dk, dv = key_dim // GLA_HEADS, val_dim // GLA_HEADS
    rank = w_gla_gk.shape[1]

    o_sc, o_cf = 0, 3 * sc_w
    o_q = o_cf + 2 * cf_w
    o_k = o_q + key_dim
    o_v = o_k + key_dim
    o_go = o_v + val_dim
    o_lr = o_go + val_dim
    o_m = o_lr + rank

    w_main = jnp.concatenate([w_in[:, :, :o_lr], w_in[:, :, o_m:]], axis=2).astype(BF16)
    w_lr = jnp.pad(w_in[:, :, o_lr:o_m], ((0, 0), (0, 0), (0, LANES - rank))).astype(BF16)
    w_gk = jnp.pad(w_gla_gk, ((0, 0), (0, LANES - rank), (0, 0))).astype(BF16)
    w_sc_out, w_cf_out, w_gla_out, w_o = (a.astype(BF16) for a in (w_sc_out, w_cf_out, w_gla_out, w_o))
    w_gate_up, w_down, w_ple_gate, w_ple = (a.astype(BF16) for a in (w_gate_up, w_down, w_ple_gate, w_ple))
    vec = lambda a: a.reshape(a.shape[0], 1, a.shape[1])
    g_mix, g_ffn, g_ple = vec(g_mix), vec(g_ffn), vec(g_ple)
    sc_conv_b, cf_conv_b, cf_ln_g, cf_ln_b = vec(sc_conv_b), vec(cf_conv_b), vec(cf_ln_g), vec(cf_ln_b)
    b_gla_gk, g_gla_norm = vec(b_gla_gk), vec(g_gla_norm)

    tm_big = min(1024, m)
    x = x.reshape(m, d)
    p = p.reshape(depth, m, p.shape[-1])
    o_m_main = o_lr
    for i in range(depth):
        z = norm_matmul(x, g_mix, w_main, i, tm=tm_big, tn=_tile(w_main.shape[-1], 1024), out_dtype=BF16)
        gk_lr = norm_matmul(x, g_mix, w_lr, i, tm=tm_big, tn=LANES, out_dtype=BF16)
        a_sc = sc_mixer(z, sc_conv_w, sc_conv_b, i, seq=seq, width=sc_w, col0=o_sc // sc_w, tm=min(512, seq))
        a_cf = cf_mixer(z, cf_conv_w, cf_conv_b, cf_ln_g, cf_ln_b, i, seq=seq, width=cf_w,
                        col0=o_cf // cf_w, tm=min(256, seq))
        a_gla = gla_mixer(z, gk_lr, w_gk, b_gla_gk, g_gla_norm, i, batch=batch, seq=seq, dk=dk, dv=dv,
                          q_col=o_q // dk, k_col=o_k // dk, v_col=o_v // dv, go_col=o_go // dv,
                          tt=min(256, seq))
        tn_merge = _tile(d, 512)
        merged = merge_mixers(a_sc, a_cf, a_gla, z, w_sc_out, w_cf_out, w_gla_out, i,
                              m_col=o_m_main // tn_merge, tm=tm_big, tn=tn_merge)
        x = matmul_residual(merged, w_o, x, i, tm=tm_big, tn=_tile(d, 1024))
        act = gate_up(x, g_ffn, w_gate_up, i, tm=tm_big, tn=_tile(w_down.shape[1], 512))
        x = matmul_residual(act, w_down, x, i, tm=tm_big, tn=_tile(d, 512))
        x = ple(x, g_ple, p, w_ple, w_ple_gate, i, tm=tm_big, tn=_tile(d, 1024))
    out = rmsnorm(x, g_final.reshape(1, d), tm=min(512, m))
    return out.reshape(batch, seq, d)
```

```python
import functools

import jax
import jax.numpy as jnp
from jax import lax
from jax.experimental import pallas as pl
from jax.experimental.pallas import tpu as pltpu

F32 = jnp.float32
BF16 = jnp.bfloat16

EPS = 1e-6
SC_KERNEL = 3
CF_KERNEL = 31
GLA_HEADS = 4
GLA_GATE_RANK = 16
GLA_GATE_NORMALIZER = 16.0
GLA_FAST_CHUNK = 128
GLA_SAFE_DECAY = 60.0
GLA_CHUNK = 64
GLA_SUB = 16
LANES = 128
BF16_SUBLANES = 16
VMEM_LIMIT_BYTES = 56 * 1024 * 1024


def _tile(n, want):
    t = min(want, n) // LANES * LANES
    while n % t:
        t -= LANES
    return t


def _params(*sem):
    return pltpu.CompilerParams(dimension_semantics=sem, vmem_limit_bytes=VMEM_LIMIT_BYTES)


def _rms_scale(x):
    return lax.rsqrt(jnp.mean(x * x, axis=-1, keepdims=True) + EPS)


def _sigmoid(x):
    return 1.0 / (1.0 + jnp.exp(-x))


def _dot(a, b):
    return jnp.dot(a, b, preferred_element_type=F32)


def _norm_matmul_kernel(x_ref, g_ref, w_ref, o_ref, h_scr):
    @pl.when(pl.program_id(1) == 0)
    def _():
        x = x_ref[...]
        h_scr[...] = (x * _rms_scale(x) * g_ref[...]).astype(BF16)

    o_ref[...] = _dot(h_scr[...], w_ref[...]).astype(o_ref.dtype)


def norm_matmul(x, g, w, layer, *, tm, tn, out_dtype):
    m, d = x.shape
    n = w.shape[-1]
    return pl.pallas_call(
        _norm_matmul_kernel,
        grid=(m // tm, n // tn),
        in_specs=[
            pl.BlockSpec((tm, d), lambda i, j: (i, 0)),
            pl.BlockSpec((None, 1, d), lambda i, j: (layer, 0, 0)),
            pl.BlockSpec((None, d, tn), lambda i, j: (layer, 0, j)),
        ],
        out_specs=pl.BlockSpec((tm, tn), lambda i, j: (i, j)),
        out_shape=jax.ShapeDtypeStruct((m, n), out_dtype),
        scratch_shapes=[pltpu.VMEM((tm, d), BF16)],
        compiler_params=_params("parallel", "arbitrary"),
        name="norm_matmul",
    )(x, g, w)


def _sc_kernel(b_ref, c_ref, x_ref, ch_ref, xh_ref, w_ref, bias_ref, o_ref, y_scr, *, tiles_per_seq):
    halo = ch_ref.shape[0]
    tm = c_ref.shape[0]
    first = pl.program_id(0) % tiles_per_seq == 0
    yh = ch_ref[...].astype(F32) * xh_ref[...].astype(F32)
    y_scr[0:halo, :] = jnp.where(first, 0.0, yh)
    y = c_ref[...].astype(F32) * x_ref[...].astype(F32)
    y_scr[halo:halo + tm, :] = y
    u = bias_ref[...] + w_ref[SC_KERNEL - 1:SC_KERNEL, :] * y
    for k in range(SC_KERNEL - 1):
        off = halo - (SC_KERNEL - 1) + k
        u = u + w_ref[k:k + 1, :] * y_scr[off:off + tm, :]
    o_ref[...] = (b_ref[...].astype(F32) * u).astype(o_ref.dtype)


def sc_mixer(z, conv_w, conv_b, layer, *, seq, width, col0, tm):
    m = z.shape[0]
    halo = BF16_SUBLANES
    r = tm // halo
    main = lambda c: pl.BlockSpec((tm, width), lambda i: (i, col0 + c))
    prev = lambda c: pl.BlockSpec((halo, width), lambda i: (jnp.maximum(i * r - 1, 0), col0 + c))
    return pl.pallas_call(
        functools.partial(_sc_kernel, tiles_per_seq=seq // tm),
        grid=(m // tm,),
        in_specs=[main(0), main(1), main(2), prev(1), prev(2),
                  pl.BlockSpec((None, SC_KERNEL, width), lambda i: (layer, 0, 0)),
                  pl.BlockSpec((None, 1, width), lambda i: (layer, 0, 0))],
        out_specs=pl.BlockSpec((tm, width), lambda i: (i, 0)),
        out_shape=jax.ShapeDtypeStruct((m, width), BF16),
        scratch_shapes=[pltpu.VMEM((tm + halo, width), F32)],
        compiler_params=_params("parallel"),
        name="sc_mixer",
    )(z, z, z, z, z, conv_w, conv_b)


def _cf_kernel(a_ref, g_ref, ah_ref, gh_ref, w_ref, bias_ref, lng_ref, lnb_ref, o_ref, y_scr,
               *, tiles_per_seq, rows):
    halo = ah_ref.shape[0]
    tm = a_ref.shape[0]
    first = pl.program_id(0) % tiles_per_seq == 0
    yh = ah_ref[...].astype(F32) * _sigmoid(gh_ref[...].astype(F32))
    y_scr[0:halo, :] = jnp.where(first, 0.0, yh)
    y_scr[halo:halo + tm, :] = a_ref[...].astype(F32) * _sigmoid(g_ref[...].astype(F32))
    for r0 in range(0, tm, rows):
        acc = jnp.broadcast_to(bias_ref[...], (rows, bias_ref.shape[1]))
        for k in range(CF_KERNEL):
            off = halo - (CF_KERNEL - 1) + k + r0
            acc = acc + w_ref[k:k + 1, :] * y_scr[off:off + rows, :]
        mu = jnp.mean(acc, axis=-1, keepdims=True)
        cen = acc - mu
        var = jnp.mean(cen * cen, axis=-1, keepdims=True)
        u = cen * lax.rsqrt(var + EPS) * lng_ref[...] + lnb_ref[...]
        o_ref[r0:r0 + rows, :] = (u * _sigmoid(u)).astype(o_ref.dtype)


def cf_mixer(z, conv_w, conv_b, ln_g, ln_b, layer, *, seq, width, col0, tm, rows=16):
    m = z.shape[0]
    halo = 2 * BF16_SUBLANES
    assert halo >= CF_KERNEL - 1
    r = tm // halo
    main = lambda c: pl.BlockSpec((tm, width), lambda i: (i, col0 + c))
    prev = lambda c: pl.BlockSpec((halo, width), lambda i: (jnp.maximum(i * r - 1, 0), col0 + c))
    vec = pl.BlockSpec((None, 1, width), lambda i: (layer, 0, 0))
    return pl.pallas_call(
        functools.partial(_cf_kernel, tiles_per_seq=seq // tm, rows=rows),
        grid=(m // tm,),
        in_specs=[main(0), main(1), prev(0), prev(1),
                  pl.BlockSpec((None, CF_KERNEL, width), lambda i: (layer, 0, 0)), vec, vec, vec],
        out_specs=pl.BlockSpec((tm, width), lambda i: (i, 0)),
        out_shape=jax.ShapeDtypeStruct((m, width), BF16),
        scratch_shapes=[pltpu.VMEM((tm + halo, width), F32)],
        compiler_params=_params("parallel"),
        name="cf_mixer",
    )(z, z, z, z, conv_w, conv_b, ln_g, ln_b)


def _chunk_cumsum(gk, chunk):
    tt = gk.shape[0]
    row = lax.broadcasted_iota(jnp.int32, (tt, tt), 0)
    col = lax.broadcasted_iota(jnp.int32, (tt, tt), 1)
    shift = chunk.bit_length() - 1
    tril = jnp.where((col <= row) & ((row >> shift) == (col >> shift)), 1.0, 0.0).astype(BF16)
    g_hi = gk.astype(BF16)
    g_lo = (gk - g_hi.astype(F32)).astype(BF16)
    return _dot(tril, g_hi) + _dot(tril, g_lo)


def _advance_state(state_scr, b_last, k_end, vc):
    dk, dv = state_scr.shape
    upd = lax.dot_general(k_end, vc, (((0,), (0,)), ((), ())), preferred_element_type=F32)
    dec_col = jnp.transpose(jnp.broadcast_to(jnp.exp(b_last), (LANES, dk)))
    for l0 in range(0, dv, LANES):
        state_scr[:, l0:l0 + LANES] = state_scr[:, l0:l0 + LANES] * dec_col + upd[:, l0:l0 + LANES]


def _gla_factorised(q, k, b, v_ref, state_scr, o_scr):
    tt = q.shape[0]
    row = lax.broadcasted_iota(jnp.int32, (GLA_FAST_CHUNK, GLA_FAST_CHUNK), 0)
    col = lax.broadcasted_iota(jnp.int32, (GLA_FAST_CHUNK, GLA_FAST_CHUNK), 1)
    for base in range(0, tt, GLA_FAST_CHUNK):
        rows = slice(base, base + GLA_FAST_CHUNK)
        bc, kc, vc = b[rows], k[rows], v_ref[rows, :]
        q_dec = (q[rows] * jnp.exp(bc)).astype(BF16)
        k_inv = (kc * jnp.exp(-bc)).astype(BF16)
        scores = lax.dot_general(q_dec, k_inv, (((1,), (1,)), ((), ())), preferred_element_type=F32)
        scores = jnp.where(col <= row, scores, 0.0).astype(BF16)
        o_scr[rows, :] = _dot(q_dec, state_scr[...].astype(BF16)) + _dot(scores, vc)
        b_last = bc[GLA_FAST_CHUNK - 1:GLA_FAST_CHUNK]
        k_end = (kc * jnp.exp(b_last - bc)).astype(BF16)
        _advance_state(state_scr, b_last, k_end, vc)


def _gla_nonpositive(q, k, gk, v_ref, state_scr, o_scr, b_scr, k_scr, q_scr):
    tt, dk = q.shape
    n_sub = GLA_CHUNK // GLA_SUB
    b_scr[...] = _chunk_cumsum(gk, GLA_CHUNK)
    k_scr[...] = k
    q_scr[...] = q
    rows_sub = lax.broadcasted_iota(jnp.int32, (GLA_SUB, dk), 0)
    rows_chunk = lax.broadcasted_iota(jnp.int32, (GLA_CHUNK, dk), 0)
    lane_chunk = lax.broadcasted_iota(jnp.int32, (GLA_SUB, GLA_CHUNK), 1)

    for base in range(0, tt, GLA_CHUNK):
        bc = b_scr[base:base + GLA_CHUNK, :]
        kc = k_scr[base:base + GLA_CHUNK, :]
        qc = q_scr[base:base + GLA_CHUNK, :]
        vc = v_ref[base:base + GLA_CHUNK, :]

        score_rows = []
        for blk in range(n_sub):
            r0 = base + blk * GLA_SUB
            q_blk = q_scr[r0:r0 + GLA_SUB, :]
            b_blk = b_scr[r0:r0 + GLA_SUB, :]

            def diag_step(j, s_acc, r0=r0, q_blk=q_blk, b_blk=b_blk, blk=blk):
                bj = b_scr[pl.ds(r0 + j, 1), :]
                kj = k_scr[pl.ds(r0 + j, 1), :]
                decay = jnp.where(rows_sub >= j, jnp.exp(jnp.minimum(b_blk - bj, 0.0)), 0.0)
                s = jnp.sum(q_blk * decay * kj, axis=-1, keepdims=True)
                return jnp.where(lane_chunk == blk * GLA_SUB + j, s, s_acc)

            s_blk = lax.fori_loop(0, GLA_SUB, diag_step, jnp.zeros((GLA_SUB, GLA_CHUNK), F32))
            if blk > 0:
                b_start = b_scr[r0 - 1:r0, :]
                k_dec = jnp.where(rows_chunk < blk * GLA_SUB,
                                  jnp.exp(jnp.minimum(b_start - bc, 0.0)), 0.0)
                k_sc = (kc * k_dec).astype(BF16)
                q_sc = (q_blk * jnp.exp(jnp.minimum(b_blk - b_start, 0.0))).astype(BF16)
                s_blk = s_blk + lax.dot_general(q_sc, k_sc, (((1,), (1,)), ((), ())),
                                                preferred_element_type=F32)
            score_rows.append(s_blk)
        scores = jnp.concatenate(score_rows, axis=0).astype(BF16)

        q_dec = (qc * jnp.exp(bc)).astype(BF16)
        o_scr[base:base + GLA_CHUNK, :] = _dot(q_dec, state_scr[...].astype(BF16)) + _dot(scores, vc)
        b_last = b_scr[base + GLA_CHUNK - 1:base + GLA_CHUNK, :]
        k_end = (kc * jnp.exp(jnp.minimum(b_last - bc, 0.0))).astype(BF16)
        _advance_state(state_scr, b_last, k_end, vc)


def _gla_kernel(q_ref, k_ref, v_ref, go_ref, lr_ref, wgk_ref, bgk_ref, gn_ref, o_ref,
                state_scr, o_scr, b_scr, k_scr, q_scr):
    dk = q_ref.shape[1]

    @pl.when(pl.program_id(2) == 0)
    def _():
        state_scr[...] = jnp.zeros_like(state_scr)

    x = _dot(lr_ref[...], wgk_ref[...]) + bgk_ref[...]
    gk = (jnp.minimum(x, 0.0) - jnp.log1p(jnp.exp(-jnp.abs(x)))) * (1.0 / GLA_GATE_NORMALIZER)
    q = q_ref[...].astype(F32) * (dk ** -0.5)
    k = k_ref[...].astype(F32)
    b = _chunk_cumsum(gk, GLA_FAST_CHUNK)
    safe = jnp.max(-b) <= GLA_SAFE_DECAY

    @pl.when(safe)
    def _():
        _gla_factorised(q, k, b, v_ref, state_scr, o_scr)

    @pl.when(jnp.logical_not(safe))
    def _():
        _gla_nonpositive(q, k, gk, v_ref, state_scr, o_scr, b_scr, k_scr, q_scr)

    o = o_scr[...]
    g = go_ref[...].astype(F32)
    o_ref[...] = (o * _rms_scale(o) * gn_ref[...] * (g * _sigmoid(g))).astype(o_ref.dtype)


def gla_mixer(z, gk_lr, w_gk, b_gk, g_norm, layer, *, batch, seq, dk, dv, q_col, k_col, v_col, go_col, tt):
    m = z.shape[0]
    nt = seq // tt
    rows = lambda b, h, t: b * nt + t
    return pl.pallas_call(
        _gla_kernel,
        grid=(batch, GLA_HEADS, nt),
        in_specs=[
            pl.BlockSpec((tt, dk), lambda b, h, t: (rows(b, h, t), q_col + h)),
            pl.BlockSpec((tt, dk), lambda b, h, t: (rows(b, h, t), k_col + h)),
            pl.BlockSpec((tt, dv), lambda b, h, t: (rows(b, h, t), v_col + h)),
            pl.BlockSpec((tt, dv), lambda b, h, t: (rows(b, h, t), go_col + h)),
            pl.BlockSpec((tt, LANES), lambda b, h, t: (rows(b, h, t), 0)),
            pl.BlockSpec((None, LANES, dk), lambda b, h, t: (layer, 0, h)),
            pl.BlockSpec((None, 1, dk), lambda b, h, t: (layer, 0, h)),
            pl.BlockSpec((None, 1, dv), lambda b, h, t: (layer, 0, 0)),
        ],
        out_specs=pl.BlockSpec((tt, dv), lambda b, h, t: (rows(b, h, t), h)),
        out_shape=jax.ShapeDtypeStruct((m, GLA_HEADS * dv), BF16),
        scratch_shapes=[pltpu.VMEM((dk, dv), F32), pltpu.VMEM((tt, dv), F32), pltpu.VMEM((tt, dk), F32),
                        pltpu.VMEM((tt, dk), F32), pltpu.VMEM((tt, dk), F32)],
        compiler_params=_params("parallel", "parallel", "arbitrary"),
        name="gla_mixer",
    )(z, z, z, z, gk_lr, w_gk, b_gk, g_norm)


def _merge_kernel(a1_ref, a2_ref, a3_ref, m1_ref, m2_ref, m3_ref, w1_ref, w2_ref, w3_ref, o_ref):
    acc = _sigmoid(m1_ref[...].astype(F32)) * _dot(a1_ref[...], w1_ref[...])
    acc = acc + _sigmoid(m2_ref[...].astype(F32)) * _dot(a2_ref[...], w2_ref[...])
    acc = acc + _sigmoid(m3_ref[...].astype(F32)) * _dot(a3_ref[...], w3_ref[...])
    o_ref[...] = acc.astype(o_ref.dtype)


def merge_mixers(a1, a2, a3, z, w1, w2, w3, layer, *, m_col, tm, tn):
    m = a1.shape[0]
    n = w1.shape[-1]
    nb = n // tn
    act = lambda a: pl.BlockSpec((tm, a.shape[1]), lambda i, j: (i, 0))
    gate = lambda c: pl.BlockSpec((tm, tn), lambda i, j: (i, m_col + c * nb + j))
    wgt = lambda w: pl.BlockSpec((None, w.shape[1], tn), lambda i, j: (layer, 0, j))
    return pl.pallas_call(
        _merge_kernel,
        grid=(m // tm, nb),
        in_specs=[act(a1), act(a2), act(a3), gate(0), gate(1), gate(2), wgt(w1), wgt(w2), wgt(w3)],
        out_specs=pl.BlockSpec((tm, tn), lambda i, j: (i, j)),
        out_shape=jax.ShapeDtypeStruct((m, n), BF16),
        compiler_params=_params("parallel", "parallel"),
        name="merge_mixers",
    )(a1, a2, a3, z, z, z, w1, w2, w3)


def _matmul_residual_kernel(a_ref, w_ref, x_ref, o_ref):
    o_ref[...] = x_ref[...] + _dot(a_ref[...], w_ref[...])


def matmul_residual(a, w, x, layer, *, tm, tn):
    m, k = a.shape
    n = w.shape[-1]
    return pl.pallas_call(
        _matmul_residual_kernel,
        grid=(m // tm, n // tn),
        in_specs=[pl.BlockSpec((tm, k), lambda i, j: (i, 0)),
                  pl.BlockSpec((None, k, tn), lambda i, j: (layer, 0, j)),
                  pl.BlockSpec((tm, tn), lambda i, j: (i, j))],
        out_specs=pl.BlockSpec((tm, tn), lambda i, j: (i, j)),
        out_shape=jax.ShapeDtypeStruct((m, n), F32),
        compiler_params=_params("parallel", "parallel"),
        name="matmul_residual",
    )(a, w, x)


def _gate_up_kernel(x_ref, g_ref, wg_ref, wu_ref, o_ref, h_scr):
    @pl.when(pl.program_id(1) == 0)
    def _():
        x = x_ref[...]
        h_scr[...] = (x * _rms_scale(x) * g_ref[...]).astype(BF16)

    h = h_scr[...]
    gate = _dot(h, wg_ref[...])
    up = _dot(h, wu_ref[...])
    o_ref[...] = (gate * _sigmoid(gate) * up).astype(o_ref.dtype)


def gate_up(x, g, w, layer, *, tm, tn):
    m, d = x.shape
    f = w.shape[-1] // 2
    nb = f // tn
    return pl.pallas_call(
        _gate_up_kernel,
        grid=(m // tm, nb),
        in_specs=[pl.BlockSpec((tm, d), lambda i, j: (i, 0)),
                  pl.BlockSpec((None, 1, d), lambda i, j: (layer, 0, 0)),
                  pl.BlockSpec((None, d, tn), lambda i, j: (layer, 0, j)),
                  pl.BlockSpec((None, d, tn), lambda i, j: (layer, 0, nb + j))],
        out_specs=pl.BlockSpec((tm, tn), lambda i, j: (i, j)),
        out_shape=jax.ShapeDtypeStruct((m, f), BF16),
        scratch_shapes=[pltpu.VMEM((tm, d), BF16)],
        compiler_params=_params("parallel", "arbitrary"),
        name="gate_up",
    )(x, g, w, w)


def _ple_kernel(x_ref, g_ref, p_ref, wp_ref, wg_ref, o_ref, h_scr, p_scr):
    tn = o_ref.shape[1]

    @pl.when(pl.program_id(1) == 0)
    def _():
        x = x_ref[...]
        h_scr[...] = (x * _rms_scale(x) * g_ref[...]).astype(BF16)
        p_scr[...] = p_ref[...].astype(BF16)

    j0 = pl.multiple_of(pl.program_id(1) * tn, tn)
    emb = _dot(p_scr[...], wp_ref[...])
    gate = _sigmoid(_dot(h_scr[...], wg_ref[...]))
    o_ref[...] = x_ref[:, pl.ds(j0, tn)] + emb * gate


def ple(x, g, p, wp, wg, layer, *, tm, tn):
    m, d = x.shape
    e = p.shape[-1]
    return pl.pallas_call(
        _ple_kernel,
        grid=(m // tm, d // tn),
        in_specs=[pl.BlockSpec((tm, d), lambda i, j: (i, 0)),
                  pl.BlockSpec((None, 1, d), lambda i, j: (layer, 0, 0)),
                  pl.BlockSpec((None, tm, e), lambda i, j: (layer, i, 0)),
                  pl.BlockSpec((None, e, tn), lambda i, j: (layer, 0, j)),
                  pl.BlockSpec((None, d, tn), lambda i, j: (layer, 0, j))],
        out_specs=pl.BlockSpec((tm, tn), lambda i, j: (i, j)),
        out_shape=jax.ShapeDtypeStruct((m, d), F32),
        scratch_shapes=[pltpu.VMEM((tm, d), BF16), pltpu.VMEM((tm, e), BF16)],
        compiler_params=_params("parallel", "arbitrary"),
        name="ple",
    )(x, g, p, wp, wg)


def _rmsnorm_kernel(x_ref, g_ref, o_ref):
    x = x_ref[...]
    o_ref[...] = x * _rms_scale(x) * g_ref[...]


def rmsnorm(x, g, *, tm):
    m, d = x.shape
    return pl.pallas_call(
        _rmsnorm_kernel,
        grid=(m // tm,),
        in_specs=[pl.BlockSpec((tm, d), lambda i: (i, 0)), pl.BlockSpec((1, d), lambda i: (0, 0))],
        out_specs=pl.BlockSpec((tm, d), lambda i: (i, 0)),
        out_shape=jax.ShapeDtypeStruct((m, d), F32),
        compiler_params=_params("parallel"),
        name="final_rmsnorm",
    )(x, g)


def kernel(x, p, g_mix, w_in, sc_conv_w, sc_conv_b, w_sc_out, cf_conv_w, cf_conv_b, cf_ln_g, cf_ln_b,
           w_cf_out, w_gla_gk, b_gla_gk, g_gla_norm, w_gla_out, w_o, g_ffn, w_gate_up, w_down, g_ple,
           w_ple_gate, w_ple, g_final):
    batch, seq, d = x.shape
    depth = p.shape[0]
    m = batch * seq
    sc_w = sc_conv_w.shape[-1]
    cf_w = cf_conv_w.shape[-1]
    key_dim = w_gla_gk.shape[-1]
    val_dim = w_gla_out.shape[1]
    dk, dv = key_dim // GLA_HEADS, val_dim // GLA_HEADS
    rank = w_gla_gk.shape[1]

    o_sc, o_cf = 0, 3 * sc_w
    o_q = o_cf + 2 * cf_w
    o_k = o_q + key_dim
    o_v = o_k + key_dim
    o_go = o_v + val_dim
    o_lr = o_go + val_dim
    o_m = o_lr + rank

    w_main = jnp.concatenate([w_in[:, :, :o_lr], w_in[:, :, o_m:]], axis=2).astype(BF16)
    w_lr = jnp.pad(w_in[:, :, o_lr:o_m], ((0, 0), (0, 0), (0, LANES - rank))).astype(BF16)
    w_gk = jnp.pad(w_gla_gk, ((0, 0), (0, LANES - rank), (0, 0))).astype(BF16)
    w_sc_out, w_cf_out, w_gla_out, w_o = (a.astype(BF16) for a in (w_sc_out, w_cf_out, w_gla_out, w_o))
    w_gate_up, w_down, w_ple_gate, w_ple = (a.astype(BF16) for a in (w_gate_up, w_down, w_ple_gate, w_ple))
    vec = lambda a: a.reshape(a.shape[0], 1, a.shape[1])
    g_mix, g_ffn, g_ple = vec(g_mix), vec(g_ffn), vec(g_ple)
    sc_conv_b, cf_conv_b, cf_ln_g, cf_ln_b = vec(sc_conv_b), vec(cf_conv_b), vec(cf_ln_g), vec(cf_ln_b)
    b_gla_gk, g_gla_norm = vec(b_gla_gk), vec(g_gla_norm)

    tm_big = min(1024, m)
    x = x.reshape(m, d)
    p = p.reshape(depth, m, p.shape[-1])
    o_m_main = o_lr
    for i in range(depth):
        z = norm_matmul(x, g_mix, w_main, i, tm=tm_big, tn=_tile(w_main.shape[-1], 1024), out_dtype=BF16)
        gk_lr = norm_matmul(x, g_mix, w_lr, i, tm=tm_big, tn=LANES, out_dtype=BF16)
        a_sc = sc_mixer(z, sc_conv_w, sc_conv_b, i, seq=seq, width=sc_w, col0=o_sc // sc_w, tm=min(512, seq))
        a_cf = cf_mixer(z, cf_conv_w, cf_conv_b, cf_ln_g, cf_ln_b, i, seq=seq, width=cf_w,
                        col0=o_cf // cf_w, tm=min(256, seq))
        a_gla = gla_mixer(z, gk_lr, w_gk, b_gla_gk, g_gla_norm, i, batch=batch, seq=seq, dk=dk, dv=dv,
                          q_col=o_q // dk, k_col=o_k // dk, v_col=o_v // dv, go_col=o_go // dv,
                          tt=min(256, seq))
        tn_merge = _tile(d, 512)
        merged = merge_mixers(a_sc, a_cf, a_gla, z, w_sc_out, w_cf_out, w_gla_out, i,
                              m_col=o_m_main // tn_merge, tm=tm_big, tn=tn_merge)
        x = matmul_residual(merged, w_o, x, i, tm=tm_big, tn=_tile(d, 1024))
        act = gate_up(x, g_ffn, w_gate_up, i, tm=tm_big, tn=_tile(w_down.shape[1], 512))
        x = matmul_residual(act, w_down, x, i, tm=tm_big, tn=_tile(d, 512))
        x = ple(x, g_ple, p, w_ple, w_ple_gate, i, tm=tm_big, tn=_tile(d, 1024))
    out = rmsnorm(x, g_final.reshape(1, d), tm=min(512, m))
    return out.reshape(batch, seq, d)
```

```python
import functools

import jax
import jax.numpy as jnp
from jax import lax
from jax.experimental import pallas as pl
from jax.experimental.pallas import tpu as pltpu

F32 = jnp.float32
BF16 = jnp.bfloat16

EPS = 1e-6
SC_KERNEL = 3
CF_KERNEL = 31
GLA_HEADS = 4
GLA_GATE_RANK = 16
GLA_GATE_NORMALIZER = 16.0
GLA_FAST_CHUNK = 128
GLA_SAFE_DECAY = 60.0
GLA_CHUNK = 64
GLA_SUB = 16
LANES = 128
F32_SUBLANES = 8
BF16_SUBLANES = 16
VMEM_LIMIT_BYTES = 56 * 1024 * 1024


def _tile(n, want):
    t = min(want, n) // LANES * LANES
    while n % t:
        t -= LANES
    return t


def _params(*sem):
    return pltpu.CompilerParams(dimension_semantics=sem, vmem_limit_bytes=VMEM_LIMIT_BYTES)


def _rms_scale(x):
    return lax.rsqrt(jnp.mean(x * x, axis=-1, keepdims=True) + EPS)


def _sigmoid(x):
    return 1.0 / (1.0 + jnp.exp(-x))


def _dot(a, b):
    return jnp.dot(a, b, preferred_element_type=F32)


def _norm_matmul_kernel(x_ref, g_ref, w_ref, o_ref, h_scr):
    @pl.when(pl.program_id(1) == 0)
    def _():
        x = x_ref[...]
        h_scr[...] = (x * _rms_scale(x) * g_ref[...]).astype(BF16)

    o_ref[...] = _dot(h_scr[...], w_ref[...]).astype(o_ref.dtype)


def norm_matmul(x, g, w, layer, *, tm, tn, out_dtype):
    m, d = x.shape
    n = w.shape[-1]
    return pl.pallas_call(
        _norm_matmul_kernel,
        grid=(m // tm, n // tn),
        in_specs=[
            pl.BlockSpec((tm, d), lambda i, j: (i, 0)),
            pl.BlockSpec((None, 1, d), lambda i, j: (layer, 0, 0)),
            pl.BlockSpec((None, d, tn), lambda i, j: (layer, 0, j)),
        ],
        out_specs=pl.BlockSpec((tm, tn), lambda i, j: (i, j)),
        out_shape=jax.ShapeDtypeStruct((m, n), out_dtype),
        scratch_shapes=[pltpu.VMEM((tm, d), BF16)],
        compiler_params=_params("parallel", "arbitrary"),
        name="norm_matmul",
    )(x, g, w)


def _sc_kernel(b_ref, c_ref, x_ref, ch_ref, xh_ref, w_ref, bias_ref, o_ref, y_scr, *, tiles_per_seq):
    halo = ch_ref.shape[0]
    tm = c_ref.shape[0]
    first = pl.program_id(0) % tiles_per_seq == 0
    yh = ch_ref[...].astype(F32) * xh_ref[...].astype(F32)
    y_scr[0:halo, :] = jnp.where(first, 0.0, yh)
    y = c_ref[...].astype(F32) * x_ref[...].astype(F32)
    y_scr[halo:halo + tm, :] = y
    u = bias_ref[...] + w_ref[SC_KERNEL - 1:SC_KERNEL, :] * y
    for k in range(SC_KERNEL - 1):
        off = halo - (SC_KERNEL - 1) + k
        u = u + w_ref[k:k + 1, :] * y_scr[off:off + tm, :]
    o_ref[...] = (b_ref[...].astype(F32) * u).astype(o_ref.dtype)


def sc_mixer(z, conv_w, conv_b, layer, *, seq, width, col0, tm):
    m = z.shape[0]
    halo = BF16_SUBLANES
    r = tm // halo
    main = lambda c: pl.BlockSpec((tm, width), lambda i: (i, col0 + c))
    prev = lambda c: pl.BlockSpec((halo, width), lambda i: (jnp.maximum(i * r - 1, 0), col0 + c))
    return pl.pallas_call(
        functools.partial(_sc_kernel, tiles_per_seq=seq // tm),
        grid=(m // tm,),
        in_specs=[main(0), main(1), main(2), prev(1), prev(2),
                  pl.BlockSpec((None, SC_KERNEL, width), lambda i: (layer, 0, 0)),
                  pl.BlockSpec((None, 1, width), lambda i: (layer, 0, 0))],
        out_specs=pl.BlockSpec((tm, width), lambda i: (i, 0)),
        out_shape=jax.ShapeDtypeStruct((m, width), BF16),
        scratch_shapes=[pltpu.VMEM((tm + halo, width), F32)],
        compiler_params=_params("parallel"),
        name="sc_mixer",
    )(z, z, z, z, z, conv_w, conv_b)


def _cf_kernel(a_ref, g_ref, ah_ref, gh_ref, w_ref, bias_ref, lng_ref, lnb_ref, o_ref, y_scr,
               *, tiles_per_seq, rows):
    halo = ah_ref.shape[0]
    tm = a_ref.shape[0]
    first = pl.program_id(0) % tiles_per_seq == 0
    yh = ah_ref[...].astype(F32) * _sigmoid(gh_ref[...].astype(F32))
    y_scr[0, 0:halo, :] = jnp.where(first, 0.0, yh)
    y_scr[0, halo:halo + tm, :] = a_ref[...].astype(F32) * _sigmoid(g_ref[...].astype(F32))
    span = tm + halo - F32_SUBLANES
    for s in range(1, F32_SUBLANES):
        y_scr[s, 0:span, :] = y_scr[0, s:s + span, :]
    for r0 in range(0, tm, rows):
        acc = jnp.broadcast_to(bias_ref[...], (rows, bias_ref.shape[1]))
        for k in range(CF_KERNEL):
            off = halo - (CF_KERNEL - 1) + k + r0
            s, base = off % F32_SUBLANES, off // F32_SUBLANES * F32_SUBLANES
            acc = acc + w_ref[k] * y_scr[s, base:base + rows, :]
        mu = jnp.mean(acc, axis=-1, keepdims=True)
        cen = acc - mu
        var = jnp.mean(cen * cen, axis=-1, keepdims=True)
        u = cen * lax.rsqrt(var + EPS) * lng_ref[...] + lnb_ref[...]
        o_ref[r0:r0 + rows, :] = (u * _sigmoid(u)).astype(o_ref.dtype)


def cf_mixer(z, conv_w, conv_b, ln_g, ln_b, layer, *, seq, width, col0, tm, rows=16):
    m = z.shape[0]
    halo = 2 * BF16_SUBLANES
    assert halo >= CF_KERNEL - 1
    r = tm // halo
    main = lambda c: pl.BlockSpec((tm, width), lambda i: (i, col0 + c))
    prev = lambda c: pl.BlockSpec((halo, width), lambda i: (jnp.maximum(i * r - 1, 0), col0 + c))
    vec = pl.BlockSpec((None, 1, width), lambda i: (layer, 0, 0))
    return pl.pallas_call(
        functools.partial(_cf_kernel, tiles_per_seq=seq // tm, rows=rows),
        grid=(m // tm,),
        in_specs=[main(0), main(1), prev(0), prev(1),
                  pl.BlockSpec((None, CF_KERNEL, rows, width), lambda i: (layer, 0, 0, 0)), vec, vec, vec],
        out_specs=pl.BlockSpec((tm, width), lambda i: (i, 0)),
        out_shape=jax.ShapeDtypeStruct((m, width), BF16),
        scratch_shapes=[pltpu.VMEM((F32_SUBLANES, tm + halo, width), F32)],
        compiler_params=_params("parallel"),
        name="cf_mixer",
    )(z, z, z, z, jnp.broadcast_to(conv_w[:, :, None, :], conv_w.shape[:2] + (rows, width)), conv_b, ln_g, ln_b)


def _chunk_cumsum(gk, chunk):
    tt = gk.shape[0]
    row = lax.broadcasted_iota(jnp.int32, (tt, tt), 0)
    col = lax.broadcasted_iota(jnp.int32, (tt, tt), 1)
    shift = chunk.bit_length() - 1
    tril = jnp.where((col <= row) & ((row >> shift) == (col >> shift)), 1.0, 0.0).astype(BF16)
    g_hi = gk.astype(BF16)
    g_lo = (gk - g_hi.astype(F32)).astype(BF16)
    return _dot(tril, g_hi) + _dot(tril, g_lo)


def _advance_state(state_scr, b_last, k_end, vc):
    dk, dv = state_scr.shape
    upd = lax.dot_general(k_end, vc, (((0,), (0,)), ((), ())), preferred_element_type=F32)
    dec_col = jnp.transpose(jnp.broadcast_to(jnp.exp(b_last), (LANES, dk)))
    for l0 in range(0, dv, LANES):
        state_scr[:, l0:l0 + LANES] = state_scr[:, l0:l0 + LANES] * dec_col + upd[:, l0:l0 + LANES]


def _gla_factorised(qs, ks, bs, v_refs, state_scr, o_scr):
    tt = qs[0].shape[0]
    row = lax.broadcasted_iota(jnp.int32, (GLA_FAST_CHUNK, GLA_FAST_CHUNK), 0)
    col = lax.broadcasted_iota(jnp.int32, (GLA_FAST_CHUNK, GLA_FAST_CHUNK), 1)
    for base in range(0, tt, GLA_FAST_CHUNK):
        rows = slice(base, base + GLA_FAST_CHUNK)
        for h, (q, k, b, v_ref) in enumerate(zip(qs, ks, bs, v_refs)):
            bc, kc, vc = b[rows], k[rows], v_ref[rows, :]
            q_dec = (q[rows] * jnp.exp(bc)).astype(BF16)
            k_inv = (kc * jnp.exp(-bc)).astype(BF16)
            scores = lax.dot_general(q_dec, k_inv, (((1,), (1,)), ((), ())), preferred_element_type=F32)
            scores = jnp.where(col <= row, scores, 0.0).astype(BF16)
            o_scr[h, rows, :] = _dot(q_dec, state_scr[h].astype(BF16)) + _dot(scores, vc)
            b_last = bc[GLA_FAST_CHUNK - 1:GLA_FAST_CHUNK]
            k_end = (kc * jnp.exp(b_last - bc)).astype(BF16)
            _advance_state(state_scr.at[h], b_last, k_end, vc)


def _gla_nonpositive(q, k, gk, v_ref, state_scr, o_scr, b_scr, k_scr, q_scr):
    tt, dk = q.shape
    n_sub = GLA_CHUNK // GLA_SUB
    b_scr[...] = _chunk_cumsum(gk, GLA_CHUNK)
    k_scr[...] = k
    q_scr[...] = q
    rows_sub = lax.broadcasted_iota(jnp.int32, (GLA_SUB, dk), 0)
    rows_chunk = lax.broadcasted_iota(jnp.int32, (GLA_CHUNK, dk), 0)
    lane_chunk = lax.broadcasted_iota(jnp.int32, (GLA_SUB, GLA_CHUNK), 1)

    for base in range(0, tt, GLA_CHUNK):
        bc = b_scr[base:base + GLA_CHUNK, :]
        kc = k_scr[base:base + GLA_CHUNK, :]
        qc = q_scr[base:base + GLA_CHUNK, :]
        vc = v_ref[base:base + GLA_CHUNK, :]

        score_rows = []
        for blk in range(n_sub):
            r0 = base + blk * GLA_SUB
            q_blk = q_scr[r0:r0 + GLA_SUB, :]
            b_blk = b_scr[r0:r0 + GLA_SUB, :]

            def diag_step(j, s_acc, r0=r0, q_blk=q_blk, b_blk=b_blk, blk=blk):
                bj = b_scr[pl.ds(r0 + j, 1), :]
                kj = k_scr[pl.ds(r0 + j, 1), :]
                decay = jnp.where(rows_sub >= j, jnp.exp(jnp.minimum(b_blk - bj, 0.0)), 0.0)
                s = jnp.sum(q_blk * decay * kj, axis=-1, keepdims=True)
                return jnp.where(lane_chunk == blk * GLA_SUB + j, s, s_acc)

            s_blk = lax.fori_loop(0, GLA_SUB, diag_step, jnp.zeros((GLA_SUB, GLA_CHUNK), F32))
            if blk > 0:
                b_start = b_scr[r0 - 1:r0, :]
                k_dec = jnp.where(rows_chunk < blk * GLA_SUB,
                                  jnp.exp(jnp.minimum(b_start - bc, 0.0)), 0.0)
                k_sc = (kc * k_dec).astype(BF16)
                q_sc = (q_blk * jnp.exp(jnp.minimum(b_blk - b_start, 0.0))).astype(BF16)
                s_blk = s_blk + lax.dot_general(q_sc, k_sc, (((1,), (1,)), ((), ())),
                                                preferred_element_type=F32)
            score_rows.append(s_blk)
        scores = jnp.concatenate(score_rows, axis=0).astype(BF16)

        q_dec = (qc * jnp.exp(bc)).astype(BF16)
        o_scr[base:base + GLA_CHUNK, :] = _dot(q_dec, state_scr[...].astype(BF16)) + _dot(scores, vc)
        b_last = b_scr[base + GLA_CHUNK - 1:base + GLA_CHUNK, :]
        k_end = (kc * jnp.exp(jnp.minimum(b_last - bc, 0.0))).astype(BF16)
        _advance_state(state_scr, b_last, k_end, vc)


def _gla_kernel(*refs, heads):
    q_refs, k_refs, v_refs, go_refs = (refs[i * heads:(i + 1) * heads] for i in range(4))
    lr_ref, wgk_ref, bgk_ref, gn_ref, o_ref, state_scr, o_scr, b_scr, k_scr, q_scr = refs[4 * heads:]
    dk = q_refs[0].shape[1]
    dv = v_refs[0].shape[1]

    @pl.when(pl.program_id(2) == 0)
    def _():
        state_scr[...] = jnp.zeros_like(state_scr)

    x = _dot(lr_ref[...], wgk_ref[...]) + bgk_ref[...]
    gk = (jnp.minimum(x, 0.0) - jnp.log1p(jnp.exp(-jnp.abs(x)))) * (1.0 / GLA_GATE_NORMALIZER)
    b = _chunk_cumsum(gk, GLA_FAST_CHUNK)
    safe = jnp.max(-b) <= GLA_SAFE_DECAY
    qs = [r[...].astype(F32) * (dk ** -0.5) for r in q_refs]
    ks = [r[...].astype(F32) for r in k_refs]
    head = lambda a, h: a[:, h * dk:(h + 1) * dk]

    @pl.when(safe)
    def _():
        _gla_factorised(qs, ks, [head(b, h) for h in range(heads)], v_refs, state_scr, o_scr)

    @pl.when(jnp.logical_not(safe))
    def _():
        for h in range(heads):
            _gla_nonpositive(qs[h], ks[h], head(gk, h), v_refs[h], state_scr.at[h], o_scr.at[h],
                             b_scr, k_scr, q_scr)

    for h in range(heads):
        o = o_scr[h]
        g = go_refs[h][...].astype(F32)
        o_ref[:, h * dv:(h + 1) * dv] = (o * _rms_scale(o) * gn_ref[...] * (g * _sigmoid(g))).astype(o_ref.dtype)


def gla_mixer(z, z_lr, w_gk, b_gk, g_norm, layer, *, batch, seq, dk, dv, q_col, k_col, v_col, go_col,
              lr_col, tt, heads):
    m = z.shape[0]
    nt = seq // tt
    rows = lambda b, t: b * nt + t
    per_head = lambda width, col0: [
        pl.BlockSpec((tt, width), lambda b, hg, t, h=h: (rows(b, t), col0 + hg * heads + h)) for h in range(heads)]
    return pl.pallas_call(
        functools.partial(_gla_kernel, heads=heads),
        grid=(batch, GLA_HEADS // heads, nt),
        in_specs=per_head(dk, q_col) + per_head(dk, k_col) + per_head(dv, v_col) + per_head(dv, go_col) + [
            pl.BlockSpec((tt, LANES), lambda b, hg, t: (rows(b, t), lr_col)),
            pl.BlockSpec((None, LANES, heads * dk), lambda b, hg, t: (layer, 0, hg)),
            pl.BlockSpec((None, 1, heads * dk), lambda b, hg, t: (layer, 0, hg)),
            pl.BlockSpec((None, 1, dv), lambda b, hg, t: (layer, 0, 0)),
        ],
        out_specs=pl.BlockSpec((tt, heads * dv), lambda b, hg, t: (rows(b, t), hg)),
        out_shape=jax.ShapeDtypeStruct((m, GLA_HEADS * dv), BF16),
        scratch_shapes=[pltpu.VMEM((heads, dk, dv), F32), pltpu.VMEM((heads, tt, dv), F32),
                        pltpu.VMEM((tt, dk), F32), pltpu.VMEM((tt, dk), F32), pltpu.VMEM((tt, dk), F32)],
        compiler_params=_params("parallel", "parallel", "arbitrary"),
        name="gla_mixer",
    )(*([z] * (4 * heads)), z_lr, w_gk, b_gk, g_norm)


def _merge_kernel(a1_ref, a2_ref, a3_ref, m1_ref, m2_ref, m3_ref, w1_ref, w2_ref, w3_ref, o_ref):
    acc = _sigmoid(m1_ref[...].astype(F32)) * _dot(a1_ref[...], w1_ref[...])
    acc = acc + _sigmoid(m2_ref[...].astype(F32)) * _dot(a2_ref[...], w2_ref[...])
    acc = acc + _sigmoid(m3_ref[...].astype(F32)) * _dot(a3_ref[...], w3_ref[...])
    o_ref[...] = acc.astype(o_ref.dtype)


def merge_mixers(a1, a2, a3, z, w1, w2, w3, layer, *, m_col, tm, tn):
    m = a1.shape[0]
    n = w1.shape[-1]
    nb = n // tn
    act = lambda a: pl.BlockSpec((tm, a.shape[1]), lambda i, j: (i, 0))
    gate = lambda c: pl.BlockSpec((tm, tn), lambda i, j: (i, m_col + c * nb + j))
    wgt = lambda w: pl.BlockSpec((None, w.shape[1], tn), lambda i, j: (layer, 0, j))
    return pl.pallas_call(
        _merge_kernel,
        grid=(m // tm, nb),
        in_specs=[act(a1), act(a2), act(a3), gate(0), gate(1), gate(2), wgt(w1), wgt(w2), wgt(w3)],
        out_specs=pl.BlockSpec((tm, tn), lambda i, j: (i, j)),
        out_shape=jax.ShapeDtypeStruct((m, n), BF16),
        compiler_params=_params("parallel", "parallel"),
        name="merge_mixers",
    )(a1, a2, a3, z, z, z, w1, w2, w3)


def _matmul_residual_kernel(a_ref, w_ref, x_ref, o_ref):
    o_ref[...] = x_ref[...] + _dot(a_ref[...], w_ref[...])


def matmul_residual(a, w, x, layer, *, tm, tn):
    m, k = a.shape
    n = w.shape[-1]
    return pl.pallas_call(
        _matmul_residual_kernel,
        grid=(m // tm, n // tn),
        in_specs=[pl.BlockSpec((tm, k), lambda i, j: (i, 0)),
                  pl.BlockSpec((None, k, tn), lambda i, j: (layer, 0, j)),
                  pl.BlockSpec((tm, tn), lambda i, j: (i, j))],
        out_specs=pl.BlockSpec((tm, tn), lambda i, j: (i, j)),
        out_shape=jax.ShapeDtypeStruct((m, n), F32),
        compiler_params=_params("parallel", "parallel"),
        name="matmul_residual",
    )(a, w, x)


def _gate_up_kernel(x_ref, g_ref, wg_ref, wu_ref, o_ref, h_scr):
    @pl.when(pl.program_id(1) == 0)
    def _():
        x = x_ref[...]
        h_scr[...] = (x * _rms_scale(x) * g_ref[...]).astype(BF16)

    h = h_scr[...]
    gate = _dot(h, wg_ref[...])
    up = _dot(h, wu_ref[...])
    o_ref[...] = (gate * _sigmoid(gate) * up).astype(o_ref.dtype)


def gate_up(x, g, w, layer, *, tm, tn):
    m, d = x.shape
    f = w.shape[-1] // 2
    nb = f // tn
    return pl.pallas_call(
        _gate_up_kernel,
        grid=(m // tm, nb),
        in_specs=[pl.BlockSpec((tm, d), lambda i, j: (i, 0)),
                  pl.BlockSpec((None, 1, d), lambda i, j: (layer, 0, 0)),
                  pl.BlockSpec((None, d, tn), lambda i, j: (layer, 0, j)),
                  pl.BlockSpec((None, d, tn), lambda i, j: (layer, 0, nb + j))],
        out_specs=pl.BlockSpec((tm, tn), lambda i, j: (i, j)),
        out_shape=jax.ShapeDtypeStruct((m, f), BF16),
        scratch_shapes=[pltpu.VMEM((tm, d), BF16)],
        compiler_params=_params("parallel", "arbitrary"),
        name="gate_up",
    )(x, g, w, w)


def _ple_kernel(x_ref, g_ref, p_ref, wp_ref, wg_ref, o_ref, h_scr, p_scr):
    tn = o_ref.shape[1]

    @pl.when(pl.program_id(1) == 0)
    def _():
        x = x_ref[...]
        h_scr[...] = (x * _rms_scale(x) * g_ref[...]).astype(BF16)
        p_scr[...] = p_ref[...].astype(BF16)

    j0 = pl.multiple_of(pl.program_id(1) * tn, tn)
    emb = _dot(p_scr[...], wp_ref[...])
    gate = _sigmoid(_dot(h_scr[...], wg_ref[...]))
    o_ref[...] = x_ref[:, pl.ds(j0, tn)] + emb * gate


def ple(x, g, p, wp, wg, layer, *, tm, tn):
    m, d = x.shape
    e = p.shape[-1]
    return pl.pallas_call(
        _ple_kernel,
        grid=(m // tm, d // tn),
        in_specs=[pl.BlockSpec((tm, d), lambda i, j: (i, 0)),
                  pl.BlockSpec((None, 1, d), lambda i, j: (layer, 0, 0)),
                  pl.BlockSpec((None, tm, e), lambda i, j: (layer, i, 0)),
                  pl.BlockSpec((None, e, tn), lambda i, j: (layer, 0, j)),
                  pl.BlockSpec((None, d, tn), lambda i, j: (layer, 0, j))],
        out_specs=pl.BlockSpec((tm, tn), lambda i, j: (i, j)),
        out_shape=jax.ShapeDtypeStruct((m, d), F32),
        scratch_shapes=[pltpu.VMEM((tm, d), BF16), pltpu.VMEM((tm, e), BF16)],
        compiler_params=_params("parallel", "arbitrary"),
        name="ple",
    )(x, g, p, wp, wg)


def _rmsnorm_kernel(x_ref, g_ref, o_ref):
    x = x_ref[...]
    o_ref[...] = x * _rms_scale(x) * g_ref[...]


def rmsnorm(x, g, *, tm):
    m, d = x.shape
    return pl.pallas_call(
        _rmsnorm_kernel,
        grid=(m // tm,),
        in_specs=[pl.BlockSpec((tm, d), lambda i: (i, 0)), pl.BlockSpec((1, d), lambda i: (0, 0))],
        out_specs=pl.BlockSpec((tm, d), lambda i: (i, 0)),
        out_shape=jax.ShapeDtypeStruct((m, d), F32),
        compiler_params=_params("parallel"),
        name="final_rmsnorm",
    )(x, g)


def kernel(x, p, g_mix, w_in, sc_conv_w, sc_conv_b, w_sc_out, cf_conv_w, cf_conv_b, cf_ln_g, cf_ln_b,
           w_cf_out, w_gla_gk, b_gla_gk, g_gla_norm, w_gla_out, w_o, g_ffn, w_gate_up, w_down, g_ple,
           w_ple_gate, w_ple, g_final):
    batch, seq, d = x.shape
    depth = p.shape[0]
    m = batch * seq
    sc_w = sc_conv_w.shape[-1]
    cf_w = cf_conv_w.shape[-1]
    key_dim = w_gla_gk.shape[-1]
    val_dim = w_gla_out.shape[1]
    dk, dv = key_dim // GLA_HEADS, val_dim // GLA_HEADS
    rank = w_gla_gk.shape[1]

    o_sc, o_cf = 0, 3 * sc_w
    o_q = o_cf + 2 * cf_w
    o_k = o_q + key_dim
    o_v = o_k + key_dim
    o_go = o_v + val_dim
    o_lr = o_go + val_dim
    o_m = o_lr + rank

    w_mix = w_in[:, :, :o_lr].astype(BF16)
    w_gate = jnp.concatenate(
        [w_in[:, :, o_m:], jnp.pad(w_in[:, :, o_lr:o_m], ((0, 0), (0, 0), (0, LANES - rank)))], axis=2).astype(BF16)
    w_gk = jnp.pad(w_gla_gk, ((0, 0), (0, LANES - rank), (0, 0))).astype(BF16)
    w_sc_out, w_cf_out, w_gla_out, w_o = (a.astype(BF16) for a in (w_sc_out, w_cf_out, w_gla_out, w_o))
    w_gate_up, w_down, w_ple_gate, w_ple = (a.astype(BF16) for a in (w_gate_up, w_down, w_ple_gate, w_ple))
    vec = lambda a: a.reshape(a.shape[0], 1, a.shape[1])
    g_mix, g_ffn, g_ple = vec(g_mix), vec(g_ffn), vec(g_ple)
    sc_conv_b, cf_conv_b, cf_ln_g, cf_ln_b = vec(sc_conv_b), vec(cf_conv_b), vec(cf_ln_g), vec(cf_ln_b)
    b_gla_gk, g_gla_norm = vec(b_gla_gk), vec(g_gla_norm)

    tm_big = min(1024, m)
    x = x.reshape(m, d)
    p = p.reshape(depth, m, p.shape[-1])
    for i in range(depth):
        z = norm_matmul(x, g_mix, w_mix, i, tm=tm_big, tn=_tile(w_mix.shape[-1], 1024), out_dtype=BF16)
        z_gate = norm_matmul(x, g_mix, w_gate, i, tm=tm_big, tn=_tile(w_gate.shape[-1], 1024), out_dtype=BF16)
        a_sc = sc_mixer(z, sc_conv_w, sc_conv_b, i, seq=seq, width=sc_w, col0=o_sc // sc_w, tm=min(512, seq))
        a_cf = cf_mixer(z, cf_conv_w, cf_conv_b, cf_ln_g, cf_ln_b, i, seq=seq, width=cf_w,
                        col0=o_cf // cf_w, tm=min(256, seq))
        a_gla = gla_mixer(z, z_gate, w_gk, b_gla_gk, g_gla_norm, i, batch=batch, seq=seq, dk=dk, dv=dv,
                          q_col=o_q // dk, k_col=o_k // dk, v_col=o_v // dv, go_col=o_go // dv,
                          lr_col=3 * d // LANES, tt=min(256, seq), heads=GLA_HEADS)
        merged = merge_mixers(a_sc, a_cf, a_gla, z_gate, w_sc_out, w_cf_out, w_gla_out, i,
                              m_col=0, tm=tm_big, tn=_tile(d, 512))
        x = matmul_residual(merged, w_o, x, i, tm=tm_big, tn=_tile(d, 1024))
        act = gate_up(x, g_ffn, w_gate_up, i, tm=tm_big, tn=_tile(w_down.shape[1], 512))
        x = matmul_residual(act, w_down, x, i, tm=tm_big, tn=_tile(d, 512))
        x = ple(x, g_ple, p, w_ple, w_ple_gate, i, tm=tm_big, tn=_tile(d, 1024))
    out = rmsnorm(x, g_final.reshape(1, d), tm=min(512, m))
    return out.reshape(batch, seq, d)
```

```python
import functools

import jax
import jax.numpy as jnp
from jax import lax
from jax.experimental import pallas as pl
from jax.experimental.pallas import tpu as pltpu

F32 = jnp.float32
BF16 = jnp.bfloat16

EPS = 1e-6
SC_KERNEL = 3
CF_KERNEL = 31
GLA_HEADS = 4
GLA_GATE_RANK = 16
GLA_GATE_NORMALIZER = 16.0
GLA_FAST_CHUNK = 128
GLA_SAFE_DECAY = 60.0
GLA_CHUNK = 64
GLA_SUB = 16
LANES = 128
F32_SUBLANES = 8
BF16_SUBLANES = 16
CONV_ROWS = 8
VMEM_LIMIT_BYTES = 56 * 1024 * 1024


def _tile(n, want):
    t = min(want, n) // LANES * LANES
    while n % t:
        t -= LANES
    return t


def _params(*sem):
    return pltpu.CompilerParams(dimension_semantics=sem, vmem_limit_bytes=VMEM_LIMIT_BYTES)


def _rms_scale(x):
    return lax.rsqrt(jnp.mean(x * x, axis=-1, keepdims=True) + EPS)


def _sigmoid(x):
    return 1.0 / (1.0 + jnp.exp(-x))


def _dot(a, b):
    return jnp.dot(a, b, preferred_element_type=F32)


def _norm_matmul_kernel(x_ref, g_ref, w_ref, o_ref, h_scr):
    @pl.when(pl.program_id(1) == 0)
    def _():
        x = x_ref[...]
        h_scr[...] = (x * _rms_scale(x) * g_ref[...]).astype(BF16)

    o_ref[...] = _dot(h_scr[...], w_ref[...]).astype(o_ref.dtype)


def norm_matmul(x, g, w, layer, *, n, tm, tn):
    m, d = x.shape
    return pl.pallas_call(
        _norm_matmul_kernel,
        grid=(m // tm, n // tn),
        in_specs=[
            pl.BlockSpec((tm, d), lambda i, j: (i, 0)),
            pl.BlockSpec((None, 1, d), lambda i, j: (layer, 0, 0)),
            pl.BlockSpec((None, d, tn), lambda i, j: (layer, 0, j)),
        ],
        out_specs=pl.BlockSpec((tm, tn), lambda i, j: (i, j)),
        out_shape=jax.ShapeDtypeStruct((m, n), BF16),
        scratch_shapes=[pltpu.VMEM((tm, d), BF16)],
        compiler_params=_params("parallel", "arbitrary"),
        name="norm_matmul",
    )(x, g, w)


def _sc_rows(first, b_ref, c_ref, x_ref, ch_ref, xh_ref, w_ref, bias_ref, o_ref, y_scr):
    halo = ch_ref.shape[0]
    tm = c_ref.shape[0]
    yh = ch_ref[...].astype(F32) * xh_ref[...].astype(F32)
    y_scr[0:halo, :] = jnp.where(first, 0.0, yh)
    y = c_ref[...].astype(F32) * x_ref[...].astype(F32)
    y_scr[halo:halo + tm, :] = y
    u = bias_ref[...] + w_ref[SC_KERNEL - 1:SC_KERNEL, :] * y
    for k in range(SC_KERNEL - 1):
        off = halo - (SC_KERNEL - 1) + k
        u = u + w_ref[k:k + 1, :] * y_scr[off:off + tm, :]
    o_ref[...] = (b_ref[...].astype(F32) * u).astype(o_ref.dtype)


def _cf_prepare(first, a_ref, g_ref, ah_ref, gh_ref, y_scr):
    halo = ah_ref.shape[0]
    tm = a_ref.shape[0]
    yh = ah_ref[...].astype(F32) * _sigmoid(gh_ref[...].astype(F32))
    y_scr[0, 0:halo, :] = jnp.where(first, 0.0, yh)
    y_scr[0, halo:halo + tm, :] = a_ref[...].astype(F32) * _sigmoid(g_ref[...].astype(F32))
    span = tm + halo - F32_SUBLANES
    for s in range(1, F32_SUBLANES):
        y_scr[s, 0:span, :] = y_scr[0, s:s + span, :]


def _cf_conv_rows(r_lo, r_hi, w_ref, bias_ref, lng_ref, lnb_ref, o_ref, y_scr):
    halo = y_scr.shape[1] - o_ref.shape[0]
    rows = w_ref.shape[1]
    for r0 in range(r_lo, r_hi, rows):
        acc = jnp.broadcast_to(bias_ref[...], (rows, bias_ref.shape[1]))
        for k in range(CF_KERNEL):
            off = halo - (CF_KERNEL - 1) + k + r0
            s, base = off % F32_SUBLANES, off // F32_SUBLANES * F32_SUBLANES
            acc = acc + w_ref[k] * y_scr[s, base:base + rows, :]
        mu = jnp.mean(acc, axis=-1, keepdims=True)
        cen = acc - mu
        var = jnp.mean(cen * cen, axis=-1, keepdims=True)
        u = cen * lax.rsqrt(var + EPS) * lng_ref[...] + lnb_ref[...]
        o_ref[r0:r0 + rows, :] = (u * _sigmoid(u)).astype(o_ref.dtype)


def _gate_conv_kernel(x_ref, g_ref, w_ref, wlr_ref,
                      scb_ref, scc_ref, scx_ref, scch_ref, scxh_ref, scw_ref, scbias_ref,
                      cfa_ref, cfg_ref, cfah_ref, cfgh_ref, cfw_ref, cfbias_ref, lng_ref, lnb_ref,
                      o_ref, lr_ref, asc_ref, acf_ref, h_scr, sc_scr, cf_scr, *, slices_per_seq):
    i, j = pl.program_id(0), pl.program_id(1)

    @pl.when(j == 0)
    def _():
        x = x_ref[...]
        h_scr[...] = (x * _rms_scale(x) * g_ref[...]).astype(BF16)
        lr_ref[...] = _dot(h_scr[...], wlr_ref[...]).astype(lr_ref.dtype)

    o_ref[...] = _dot(h_scr[...], w_ref[...]).astype(o_ref.dtype)
    first = (i * pl.num_programs(1) + j) % slices_per_seq == 0
    _sc_rows(first, scb_ref, scc_ref, scx_ref, scch_ref, scxh_ref, scw_ref, scbias_ref, asc_ref, sc_scr)
    _cf_prepare(first, cfa_ref, cfg_ref, cfah_ref, cfgh_ref, cf_scr)
    _cf_conv_rows(0, acf_ref.shape[0], cfw_ref, cfbias_ref, lng_ref, lnb_ref, acf_ref, cf_scr)


def gate_proj_and_convs(x, g, w_gate, w_lr, z, sc_w, sc_b, cf_w, cf_b, ln_g, ln_b, layer,
                        *, seq, sc_width, sc_col, cf_width, cf_col, tm, tn):
    m, d = x.shape
    n = w_gate.shape[-1]
    nb = n // tn
    rs = tm // nb
    sc_halo, cf_halo = BF16_SUBLANES, 2 * BF16_SUBLANES
    assert tm % nb == 0 and rs % cf_halo == 0 and seq % rs == 0 and cf_halo >= CF_KERNEL - 1
    rows = lambda wd, col: pl.BlockSpec((rs, wd), lambda i, j: (i * nb + j, col))
    prev = lambda halo, wd, col: pl.BlockSpec(
        (halo, wd), lambda i, j: (jnp.maximum((i * nb + j) * (rs // halo) - 1, 0), col))
    vec = lambda wd: pl.BlockSpec((None, 1, wd), lambda i, j: (layer, 0, 0))
    cf_w_rows = jnp.broadcast_to(cf_w[:, :, None, :], cf_w.shape[:2] + (CONV_ROWS, cf_width))
    return pl.pallas_call(
        functools.partial(_gate_conv_kernel, slices_per_seq=seq // rs),
        grid=(m // tm, nb),
        in_specs=[
            pl.BlockSpec((tm, d), lambda i, j: (i, 0)),
            pl.BlockSpec((None, 1, d), lambda i, j: (layer, 0, 0)),
            pl.BlockSpec((None, d, tn), lambda i, j: (layer, 0, j)),
            pl.BlockSpec((None, d, LANES), lambda i, j: (layer, 0, 0)),
            rows(sc_width, sc_col), rows(sc_width, sc_col + 1), rows(sc_width, sc_col + 2),
            prev(sc_halo, sc_width, sc_col + 1), prev(sc_halo, sc_width, sc_col + 2),
            pl.BlockSpec((None, SC_KERNEL, sc_width), lambda i, j: (layer, 0, 0)), vec(sc_width),
            rows(cf_width, cf_col), rows(cf_width, cf_col + 1),
            prev(cf_halo, cf_width, cf_col), prev(cf_halo, cf_width, cf_col + 1),
            pl.BlockSpec((None, CF_KERNEL, CONV_ROWS, cf_width), lambda i, j: (layer, 0, 0, 0)),
            vec(cf_width), vec(cf_width), vec(cf_width),
        ],
        out_specs=[
            pl.BlockSpec((tm, tn), lambda i, j: (i, j)),
            pl.BlockSpec((tm, LANES), lambda i, j: (i, 0)),
            pl.BlockSpec((rs, sc_width), lambda i, j: (i * nb + j, 0)),
            pl.BlockSpec((rs, cf_width), lambda i, j: (i * nb + j, 0)),
        ],
        out_shape=[jax.ShapeDtypeStruct((m, n), BF16), jax.ShapeDtypeStruct((m, LANES), BF16),
                   jax.ShapeDtypeStruct((m, sc_width), BF16), jax.ShapeDtypeStruct((m, cf_width), BF16)],
        scratch_shapes=[pltpu.VMEM((tm, d), BF16), pltpu.VMEM((rs + sc_halo, sc_width), F32),
                        pltpu.VMEM((F32_SUBLANES, rs + cf_halo, cf_width), F32)],
        compiler_params=_params("parallel", "arbitrary"),
        name="gate_proj_and_convs",
    )(x, g, w_gate, w_lr, z, z, z, z, z, sc_w, sc_b, z, z, z, z, cf_w_rows, cf_b, ln_g, ln_b)


def _chunk_cumsum(gk, chunk):
    tt = gk.shape[0]
    row = lax.broadcasted_iota(jnp.int32, (tt, tt), 0)
    col = lax.broadcasted_iota(jnp.int32, (tt, tt), 1)
    shift = chunk.bit_length() - 1
    tril = jnp.where((col <= row) & ((row >> shift) == (col >> shift)), 1.0, 0.0).astype(BF16)
    g_hi = gk.astype(BF16)
    g_lo = (gk - g_hi.astype(F32)).astype(BF16)
    return _dot(tril, g_hi) + _dot(tril, g_lo)


def _advance_state(state_scr, b_last, k_end, vc):
    dk, dv = state_scr.shape
    upd = lax.dot_general(k_end, vc, (((0,), (0,)), ((), ())), preferred_element_type=F32)
    dec_col = jnp.transpose(jnp.broadcast_to(jnp.exp(b_last), (LANES, dk)))
    for l0 in range(0, dv, LANES):
        state_scr[:, l0:l0 + LANES] = state_scr[:, l0:l0 + LANES] * dec_col + upd[:, l0:l0 + LANES]


def _gla_factorised(qs, ks, bs, v_refs, state_scr, o_scr):
    tt = qs[0].shape[0]
    row = lax.broadcasted_iota(jnp.int32, (GLA_FAST_CHUNK, GLA_FAST_CHUNK), 0)
    col = lax.broadcasted_iota(jnp.int32, (GLA_FAST_CHUNK, GLA_FAST_CHUNK), 1)
    for base in range(0, tt, GLA_FAST_CHUNK):
        rows = slice(base, base + GLA_FAST_CHUNK)
        for h, (q, k, b, v_ref) in enumerate(zip(qs, ks, bs, v_refs)):
            bc, kc, vc = b[rows], k[rows], v_ref[rows, :]
            q_dec = (q[rows] * jnp.exp(bc)).astype(BF16)
            k_inv = (kc * jnp.exp(-bc)).astype(BF16)
            scores = lax.dot_general(q_dec, k_inv, (((1,), (1,)), ((), ())), preferred_element_type=F32)
            scores = jnp.where(col <= row, scores, 0.0).astype(BF16)
            o_scr[h, rows, :] = _dot(q_dec, state_scr[h].astype(BF16)) + _dot(scores, vc)
            b_last = bc[GLA_FAST_CHUNK - 1:GLA_FAST_CHUNK]
            k_end = (kc * jnp.exp(b_last - bc)).astype(BF16)
            _advance_state(state_scr.at[h], b_last, k_end, vc)


def _gla_nonpositive(q, k, gk, v_ref, state_scr, o_scr, b_scr, k_scr, q_scr):
    tt, dk = q.shape
    n_sub = GLA_CHUNK // GLA_SUB
    b_scr[...] = _chunk_cumsum(gk, GLA_CHUNK)
    k_scr[...] = k
    q_scr[...] = q
    rows_sub = lax.broadcasted_iota(jnp.int32, (GLA_SUB, dk), 0)
    rows_chunk = lax.broadcasted_iota(jnp.int32, (GLA_CHUNK, dk), 0)
    lane_chunk = lax.broadcasted_iota(jnp.int32, (GLA_SUB, GLA_CHUNK), 1)

    for base in range(0, tt, GLA_CHUNK):
        bc = b_scr[base:base + GLA_CHUNK, :]
        kc = k_scr[base:base + GLA_CHUNK, :]
        qc = q_scr[base:base + GLA_CHUNK, :]
        vc = v_ref[base:base + GLA_CHUNK, :]

        score_rows = []
        for blk in range(n_sub):
            r0 = base + blk * GLA_SUB
            q_blk = q_scr[r0:r0 + GLA_SUB, :]
            b_blk = b_scr[r0:r0 + GLA_SUB, :]

            def diag_step(j, s_acc, r0=r0, q_blk=q_blk, b_blk=b_blk, blk=blk):
                bj = b_scr[pl.ds(r0 + j, 1), :]
                kj = k_scr[pl.ds(r0 + j, 1), :]
                decay = jnp.where(rows_sub >= j, jnp.exp(jnp.minimum(b_blk - bj, 0.0)), 0.0)
                s = jnp.sum(q_blk * decay * kj, axis=-1, keepdims=True)
                return jnp.where(lane_chunk == blk * GLA_SUB + j, s, s_acc)

            s_blk = lax.fori_loop(0, GLA_SUB, diag_step, jnp.zeros((GLA_SUB, GLA_CHUNK), F32))
            if blk > 0:
                b_start = b_scr[r0 - 1:r0, :]
                k_dec = jnp.where(rows_chunk < blk * GLA_SUB,
                                  jnp.exp(jnp.minimum(b_start - bc, 0.0)), 0.0)
                k_sc = (kc * k_dec).astype(BF16)
                q_sc = (q_blk * jnp.exp(jnp.minimum(b_blk - b_start, 0.0))).astype(BF16)
                s_blk = s_blk + lax.dot_general(q_sc, k_sc, (((1,), (1,)), ((), ())),
                                                preferred_element_type=F32)
            score_rows.append(s_blk)
        scores = jnp.concatenate(score_rows, axis=0).astype(BF16)

        q_dec = (qc * jnp.exp(bc)).astype(BF16)
        o_scr[base:base + GLA_CHUNK, :] = _dot(q_dec, state_scr[...].astype(BF16)) + _dot(scores, vc)
        b_last = b_scr[base + GLA_CHUNK - 1:base + GLA_CHUNK, :]
        k_end = (kc * jnp.exp(jnp.minimum(b_last - bc, 0.0))).astype(BF16)
        _advance_state(state_scr, b_last, k_end, vc)


def _gla_kernel(*refs, heads):
    q_refs, k_refs, v_refs, go_refs = (refs[i * heads:(i + 1) * heads] for i in range(4))
    lr_ref, wgk_ref, bgk_ref, gn_ref, o_ref, state_scr, o_scr, b_scr, k_scr, q_scr = refs[4 * heads:]
    dk = q_refs[0].shape[1]
    dv = v_refs[0].shape[1]

    @pl.when(pl.program_id(2) == 0)
    def _():
        state_scr[...] = jnp.zeros_like(state_scr)

    x = _dot(lr_ref[...], wgk_ref[...]) + bgk_ref[...]
    gk = (jnp.minimum(x, 0.0) - jnp.log1p(jnp.exp(-jnp.abs(x)))) * (1.0 / GLA_GATE_NORMALIZER)
    b = _chunk_cumsum(gk, GLA_FAST_CHUNK)
    safe = jnp.max(-b) <= GLA_SAFE_DECAY
    qs = [r[...].astype(F32) * (dk ** -0.5) for r in q_refs]
    ks = [r[...].astype(F32) for r in k_refs]
    head = lambda a, h: a[:, h * dk:(h + 1) * dk]

    @pl.when(safe)
    def _():
        _gla_factorised(qs, ks, [head(b, h) for h in range(heads)], v_refs, state_scr, o_scr)

    @pl.when(jnp.logical_not(safe))
    def _():
        for h in range(heads):
            _gla_nonpositive(qs[h], ks[h], head(gk, h), v_refs[h], state_scr.at[h], o_scr.at[h],
                             b_scr, k_scr, q_scr)

    for h in range(heads):
        o = o_scr[h]
        g = go_refs[h][...].astype(F32)
        o_ref[:, h * dv:(h + 1) * dv] = (o * _rms_scale(o) * gn_ref[...] * (g * _sigmoid(g))).astype(o_ref.dtype)


def gla_mixer(z, z_lr, w_gk, b_gk, g_norm, layer, *, batch, seq, dk, dv, q_col, k_col, v_col, go_col,
              lr_col, tt, heads):
    m = z.shape[0]
    nt = seq // tt
    rows = lambda b, t: b * nt + t
    per_head = lambda width, col0: [
        pl.BlockSpec((tt, width), lambda b, hg, t, h=h: (rows(b, t), col0 + hg * heads + h)) for h in range(heads)]
    return pl.pallas_call(
        functools.partial(_gla_kernel, heads=heads),
        grid=(batch, GLA_HEADS // heads, nt),
        in_specs=per_head(dk, q_col) + per_head(dk, k_col) + per_head(dv, v_col) + per_head(dv, go_col) + [
            pl.BlockSpec((tt, LANES), lambda b, hg, t: (rows(b, t), lr_col)),
            pl.BlockSpec((None, LANES, heads * dk), lambda b, hg, t: (layer, 0, hg)),
            pl.BlockSpec((None, 1, heads * dk), lambda b, hg, t: (layer, 0, hg)),
            pl.BlockSpec((None, 1, dv), lambda b, hg, t: (layer, 0, 0)),
        ],
        out_specs=pl.BlockSpec((tt, heads * dv), lambda b, hg, t: (rows(b, t), hg)),
        out_shape=jax.ShapeDtypeStruct((m, GLA_HEADS * dv), BF16),
        scratch_shapes=[pltpu.VMEM((heads, dk, dv), F32), pltpu.VMEM((heads, tt, dv), F32),
                        pltpu.VMEM((tt, dk), F32), pltpu.VMEM((tt, dk), F32), pltpu.VMEM((tt, dk), F32)],
        compiler_params=_params("parallel", "parallel", "arbitrary"),
        name="gla_mixer",
    )(*([z] * (4 * heads)), z_lr, w_gk, b_gk, g_norm)


def _merge_kernel(a1_ref, a2_ref, a3_ref, m1_ref, m2_ref, m3_ref, w1_ref, w2_ref, w3_ref, o_ref):
    acc = _sigmoid(m1_ref[...].astype(F32)) * _dot(a1_ref[...], w1_ref[...])
    acc = acc + _sigmoid(m2_ref[...].astype(F32)) * _dot(a2_ref[...], w2_ref[...])
    acc = acc + _sigmoid(m3_ref[...].astype(F32)) * _dot(a3_ref[...], w3_ref[...])
    o_ref[...] = acc.astype(o_ref.dtype)


def merge_mixers(a1, a2, a3, z, w1, w2, w3, layer, *, m_col, tm, tn):
    m = a1.shape[0]
    n = w1.shape[-1]
    nb = n // tn
    act = lambda a: pl.BlockSpec((tm, a.shape[1]), lambda i, j: (i, 0))
    gate = lambda c: pl.BlockSpec((tm, tn), lambda i, j: (i, m_col + c * nb + j))
    wgt = lambda w: pl.BlockSpec((None, w.shape[1], tn), lambda i, j: (layer, 0, j))
    return pl.pallas_call(
        _merge_kernel,
        grid=(m // tm, nb),
        in_specs=[act(a1), act(a2), act(a3), gate(0), gate(1), gate(2), wgt(w1), wgt(w2), wgt(w3)],
        out_specs=pl.BlockSpec((tm, tn), lambda i, j: (i, j)),
        out_shape=jax.ShapeDtypeStruct((m, n), BF16),
        compiler_params=_params("parallel", "parallel"),
        name="merge_mixers",
    )(a1, a2, a3, z, z, z, w1, w2, w3)


def _matmul_residual_kernel(a_ref, w_ref, x_ref, o_ref):
    o_ref[...] = x_ref[...] + _dot(a_ref[...], w_ref[...])


def matmul_residual(a, w, x, layer, *, tm, tn):
    m, k = a.shape
    n = w.shape[-1]
    return pl.pallas_call(
        _matmul_residual_kernel,
        grid=(m // tm, n // tn),
        in_specs=[pl.BlockSpec((tm, k), lambda i, j: (i, 0)),
                  pl.BlockSpec((None, k, tn), lambda i, j: (layer, 0, j)),
                  pl.BlockSpec((tm, tn), lambda i, j: (i, j))],
        out_specs=pl.BlockSpec((tm, tn), lambda i, j: (i, j)),
        out_shape=jax.ShapeDtypeStruct((m, n), F32),
        compiler_params=_params("parallel", "parallel"),
        name="matmul_residual",
    )(a, w, x)


def _gate_up_kernel(x_ref, g_ref, wg_ref, wu_ref, o_ref, h_scr):
    @pl.when(pl.program_id(1) == 0)
    def _():
        x = x_ref[...]
        h_scr[...] = (x * _rms_scale(x) * g_ref[...]).astype(BF16)

    h = h_scr[...]
    gate = _dot(h, wg_ref[...])
    up = _dot(h, wu_ref[...])
    o_ref[...] = (gate * _sigmoid(gate) * up).astype(o_ref.dtype)


def gate_up(x, g, w, layer, *, tm, tn):
    m, d = x.shape
    f = w.shape[-1] // 2
    nb = f // tn
    return pl.pallas_call(
        _gate_up_kernel,
        grid=(m // tm, nb),
        in_specs=[pl.BlockSpec((tm, d), lambda i, j: (i, 0)),
                  pl.BlockSpec((None, 1, d), lambda i, j: (layer, 0, 0)),
                  pl.BlockSpec((None, d, tn), lambda i, j: (layer, 0, j)),
                  pl.BlockSpec((None, d, tn), lambda i, j: (layer, 0, nb + j))],
        out_specs=pl.BlockSpec((tm, tn), lambda i, j: (i, j)),
        out_shape=jax.ShapeDtypeStruct((m, f), BF16),
        scratch_shapes=[pltpu.VMEM((tm, d), BF16)],
        compiler_params=_params("parallel", "arbitrary"),
        name="gate_up",
    )(x, g, w, w)


def _ple_kernel(x_ref, g_ref, p_ref, wp_ref, wg_ref, o_ref, h_scr, p_scr):
    tn = o_ref.shape[1]

    @pl.when(pl.program_id(1) == 0)
    def _():
        x = x_ref[...]
        h_scr[...] = (x * _rms_scale(x) * g_ref[...]).astype(BF16)
        p_scr[...] = p_ref[...].astype(BF16)

    j0 = pl.multiple_of(pl.program_id(1) * tn, tn)
    emb = _dot(p_scr[...], wp_ref[...])
    gate = _sigmoid(_dot(h_scr[...], wg_ref[...]))
    o_ref[...] = x_ref[:, pl.ds(j0, tn)] + emb * gate


def ple(x, g, p, wp, wg, layer, *, tm, tn):
    m, d = x.shape
    e = p.shape[-1]
    return pl.pallas_call(
        _ple_kernel,
        grid=(m // tm, d // tn),
        in_specs=[pl.BlockSpec((tm, d), lambda i, j: (i, 0)),
                  pl.BlockSpec((None, 1, d), lambda i, j: (layer, 0, 0)),
                  pl.BlockSpec((None, tm, e), lambda i, j: (layer, i, 0)),
                  pl.BlockSpec((None, e, tn), lambda i, j: (layer, 0, j)),
                  pl.BlockSpec((None, d, tn), lambda i, j: (layer, 0, j))],
        out_specs=pl.BlockSpec((tm, tn), lambda i, j: (i, j)),
        out_shape=jax.ShapeDtypeStruct((m, d), F32),
        scratch_shapes=[pltpu.VMEM((tm, d), BF16), pltpu.VMEM((tm, e), BF16)],
        compiler_params=_params("parallel", "arbitrary"),
        name="ple",
    )(x, g, p, wp, wg)


def _rmsnorm_kernel(x_ref, g_ref, o_ref):
    x = x_ref[...]
    o_ref[...] = x * _rms_scale(x) * g_ref[...]


def rmsnorm(x, g, *, tm):
    m, d = x.shape
    return pl.pallas_call(
        _rmsnorm_kernel,
        grid=(m // tm,),
        in_specs=[pl.BlockSpec((tm, d), lambda i: (i, 0)), pl.BlockSpec((1, d), lambda i: (0, 0))],
        out_specs=pl.BlockSpec((tm, d), lambda i: (i, 0)),
        out_shape=jax.ShapeDtypeStruct((m, d), F32),
        compiler_params=_params("parallel"),
        name="final_rmsnorm",
    )(x, g)


def kernel(x, p, g_mix, w_in, sc_conv_w, sc_conv_b, w_sc_out, cf_conv_w, cf_conv_b, cf_ln_g, cf_ln_b,
           w_cf_out, w_gla_gk, b_gla_gk, g_gla_norm, w_gla_out, w_o, g_ffn, w_gate_up, w_down, g_ple,
           w_ple_gate, w_ple, g_final):
    batch, seq, d = x.shape
    depth = p.shape[0]
    m = batch * seq
    sc_w = sc_conv_w.shape[-1]
    cf_w = cf_conv_w.shape[-1]
    key_dim = w_gla_gk.shape[-1]
    val_dim = w_gla_out.shape[1]
    dk, dv = key_dim // GLA_HEADS, val_dim // GLA_HEADS
    rank = w_gla_gk.shape[1]

    o_sc, o_cf = 0, 3 * sc_w
    o_q = o_cf + 2 * cf_w
    o_k = o_q + key_dim
    o_v = o_k + key_dim
    o_go = o_v + val_dim
    o_lr = o_go + val_dim
    o_m = o_lr + rank

    w_in = w_in.astype(BF16)
    w_gate = w_in[:, :, o_m:]
    w_lr = jnp.pad(w_in[:, :, o_lr:o_m], ((0, 0), (0, 0), (0, LANES - rank)))
    w_gk = jnp.pad(w_gla_gk, ((0, 0), (0, LANES - rank), (0, 0))).astype(BF16)
    w_sc_out, w_cf_out, w_gla_out, w_o = (a.astype(BF16) for a in (w_sc_out, w_cf_out, w_gla_out, w_o))
    w_gate_up, w_down, w_ple_gate, w_ple = (a.astype(BF16) for a in (w_gate_up, w_down, w_ple_gate, w_ple))
    vec = lambda a: a.reshape(a.shape[0], 1, a.shape[1])
    g_mix, g_ffn, g_ple = vec(g_mix), vec(g_ffn), vec(g_ple)
    sc_conv_b, cf_conv_b, cf_ln_g, cf_ln_b = vec(sc_conv_b), vec(cf_conv_b), vec(cf_ln_g), vec(cf_ln_b)
    b_gla_gk, g_gla_norm = vec(b_gla_gk), vec(g_gla_norm)

    tm_big = min(1024, m)
    x = x.reshape(m, d)
    p = p.reshape(depth, m, p.shape[-1])
    for i in range(depth):
        z = norm_matmul(x, g_mix, w_in, i, n=o_lr, tm=tm_big, tn=_tile(o_lr, 1024))
        z_gate, z_lr, a_sc, a_cf = gate_proj_and_convs(
            x, g_mix, w_gate, w_lr, z, sc_conv_w, sc_conv_b, cf_conv_w, cf_conv_b, cf_ln_g, cf_ln_b, i,
            seq=seq, sc_width=sc_w, sc_col=o_sc // sc_w, cf_width=cf_w, cf_col=o_cf // cf_w,
            tm=tm_big, tn=_tile(3 * d, 768))
        a_gla = gla_mixer(z, z_lr, w_gk, b_gla_gk, g_gla_norm, i, batch=batch, seq=seq, dk=dk, dv=dv,
                          q_col=o_q // dk, k_col=o_k // dk, v_col=o_v // dv, go_col=o_go // dv,
                          lr_col=0, tt=min(512, seq), heads=GLA_HEADS)
        merged = merge_mixers(a_sc, a_cf, a_gla, z_gate, w_sc_out, w_cf_out, w_gla_out, i,
                              m_col=0, tm=tm_big, tn=_tile(d, 512))
        x = matmul_residual(merged, w_o, x, i, tm=tm_big, tn=_tile(d, 1024))
        act = gate_up(x, g_ffn, w_gate_up, i, tm=tm_big, tn=_tile(w_down.shape[1], 512))
        x = matmul_residual(act, w_down, x, i, tm=tm_big, tn=_tile(d, 512))
        x = ple(x, g_ple, p, w_ple, w_ple_gate, i, tm=tm_big, tn=_tile(d, 1024))
    out = rmsnorm(x, g_final.reshape(1, d), tm=min(512, m))
    return out.reshape(batch, seq, d)
```

```python
import functools

import jax
import jax.numpy as jnp
from jax import lax
from jax.experimental import pallas as pl
from jax.experimental.pallas import tpu as pltpu

F32 = jnp.float32
BF16 = jnp.bfloat16

EPS = 1e-6
SC_KERNEL = 3
CF_KERNEL = 31
GLA_HEADS = 4
GLA_GATE_RANK = 16
GLA_GATE_NORMALIZER = 16.0
GLA_FAST_CHUNK = 128
GLA_SAFE_DECAY = 60.0
GLA_CHUNK = 64
GLA_SUB = 16
LANES = 128
F32_SUBLANES = 8
BF16_SUBLANES = 16
CONV_ROWS = 8
VMEM_LIMIT_BYTES = 56 * 1024 * 1024


def _tile(n, want):
    t = min(want, n) // LANES * LANES
    while n % t:
        t -= LANES
    return t


def _params(*sem):
    return pltpu.CompilerParams(dimension_semantics=sem, vmem_limit_bytes=VMEM_LIMIT_BYTES)


def _rms_scale(x):
    return lax.rsqrt(jnp.mean(x * x, axis=-1, keepdims=True) + EPS)


def _sigmoid(x):
    return 1.0 / (1.0 + jnp.exp(-x))


def _dot(a, b):
    return jnp.dot(a, b, preferred_element_type=F32)


def _wdot(a, w_ref):
    return _dot(a, w_ref[...].astype(BF16))


def _norm_matmul_kernel(x_ref, g_ref, w_ref, o_ref, h_scr):
    @pl.when(pl.program_id(1) == 0)
    def _():
        x = x_ref[...]
        h_scr[...] = (x * _rms_scale(x) * g_ref[...]).astype(BF16)

    o_ref[...] = _wdot(h_scr[...], w_ref).astype(o_ref.dtype)


def norm_matmul(x, g, w, layer, *, n, tm, tn):
    m, d = x.shape
    return pl.pallas_call(
        _norm_matmul_kernel,
        grid=(m // tm, n // tn),
        in_specs=[
            pl.BlockSpec((tm, d), lambda i, j: (i, 0)),
            pl.BlockSpec((None, 1, d), lambda i, j: (layer, 0, 0)),
            pl.BlockSpec((None, d, tn), lambda i, j: (layer, 0, j)),
        ],
        out_specs=pl.BlockSpec((tm, tn), lambda i, j: (i, j)),
        out_shape=jax.ShapeDtypeStruct((m, n), BF16),
        scratch_shapes=[pltpu.VMEM((tm, d), BF16)],
        compiler_params=_params("parallel", "arbitrary"),
        name="norm_matmul",
    )(x, g, w)


def _sc_rows(first, b_ref, c_ref, x_ref, ch_ref, xh_ref, w_ref, bias_ref, o_ref, y_scr):
    halo = ch_ref.shape[0]
    tm = c_ref.shape[0]
    yh = ch_ref[...].astype(F32) * xh_ref[...].astype(F32)
    y_scr[0:halo, :] = jnp.where(first, 0.0, yh)
    y = c_ref[...].astype(F32) * x_ref[...].astype(F32)
    y_scr[halo:halo + tm, :] = y
    u = bias_ref[...] + w_ref[SC_KERNEL - 1:SC_KERNEL, :] * y
    for k in range(SC_KERNEL - 1):
        off = halo - (SC_KERNEL - 1) + k
        u = u + w_ref[k:k + 1, :] * y_scr[off:off + tm, :]
    o_ref[...] = (b_ref[...].astype(F32) * u).astype(o_ref.dtype)


def _cf_prepare(first, a_ref, g_ref, ah_ref, gh_ref, y_scr):
    halo = ah_ref.shape[0]
    tm = a_ref.shape[0]
    yh = ah_ref[...].astype(F32) * _sigmoid(gh_ref[...].astype(F32))
    y_scr[0, 0:halo, :] = jnp.where(first, 0.0, yh)
    y_scr[0, halo:halo + tm, :] = a_ref[...].astype(F32) * _sigmoid(g_ref[...].astype(F32))
    span = tm + halo - F32_SUBLANES
    for s in range(1, F32_SUBLANES):
        y_scr[s, 0:span, :] = y_scr[0, s:s + span, :]


def _cf_conv_rows(r_lo, r_hi, w_ref, bias_ref, lng_ref, lnb_ref, o_ref, y_scr):
    halo = y_scr.shape[1] - o_ref.shape[0]
    rows = w_ref.shape[1]
    for r0 in range(r_lo, r_hi, rows):
        acc = jnp.broadcast_to(bias_ref[...], (rows, bias_ref.shape[1]))
        for k in range(CF_KERNEL):
            off = halo - (CF_KERNEL - 1) + k + r0
            s, base = off % F32_SUBLANES, off // F32_SUBLANES * F32_SUBLANES
            acc = acc + w_ref[k] * y_scr[s, base:base + rows, :]
        mu = jnp.mean(acc, axis=-1, keepdims=True)
        cen = acc - mu
        var = jnp.mean(cen * cen, axis=-1, keepdims=True)
        u = cen * lax.rsqrt(var + EPS) * lng_ref[...] + lnb_ref[...]
        o_ref[r0:r0 + rows, :] = (u * _sigmoid(u)).astype(o_ref.dtype)


def _gate_conv_kernel(x_ref, g_ref, w_ref, wlr_ref,
                      scb_ref, scc_ref, scx_ref, scch_ref, scxh_ref, scw_ref, scbias_ref,
                      cfa_ref, cfg_ref, cfah_ref, cfgh_ref, cfw_ref, cfbias_ref, lng_ref, lnb_ref,
                      o_ref, lr_ref, asc_ref, acf_ref, h_scr, sc_scr, cf_scr, *, slices_per_seq):
    i, j = pl.program_id(0), pl.program_id(1)

    @pl.when(j == 0)
    def _():
        x = x_ref[...]
        h_scr[...] = (x * _rms_scale(x) * g_ref[...]).astype(BF16)
        lr_ref[...] = _dot(h_scr[...], wlr_ref[...]).astype(lr_ref.dtype)

    o_ref[...] = _dot(h_scr[...], w_ref[...]).astype(o_ref.dtype)
    first = (i * pl.num_programs(1) + j) % slices_per_seq == 0
    _sc_rows(first, scb_ref, scc_ref, scx_ref, scch_ref, scxh_ref, scw_ref, scbias_ref, asc_ref, sc_scr)
    _cf_prepare(first, cfa_ref, cfg_ref, cfah_ref, cfgh_ref, cf_scr)
    _cf_conv_rows(0, acf_ref.shape[0], cfw_ref, cfbias_ref, lng_ref, lnb_ref, acf_ref, cf_scr)


def gate_proj_and_convs(x, g, w_gate, w_lr, z, sc_w, sc_b, cf_w, cf_b, ln_g, ln_b, layer,
                        *, seq, sc_width, sc_col, cf_width, cf_col, tm, tn):
    m, d = x.shape
    n = w_gate.shape[-1]
    nb = n // tn
    rs = tm // nb
    sc_halo, cf_halo = BF16_SUBLANES, 2 * BF16_SUBLANES
    assert tm % nb == 0 and rs % cf_halo == 0 and seq % rs == 0 and cf_halo >= CF_KERNEL - 1
    rows = lambda wd, col: pl.BlockSpec((rs, wd), lambda i, j: (i * nb + j, col))
    prev = lambda halo, wd, col: pl.BlockSpec(
        (halo, wd), lambda i, j: (jnp.maximum((i * nb + j) * (rs // halo) - 1, 0), col))
    vec = lambda wd: pl.BlockSpec((None, 1, wd), lambda i, j: (layer, 0, 0))
    cf_w_rows = jnp.broadcast_to(cf_w[:, :, None, :], cf_w.shape[:2] + (CONV_ROWS, cf_width))
    return pl.pallas_call(
        functools.partial(_gate_conv_kernel, slices_per_seq=seq // rs),
        grid=(m // tm, nb),
        in_specs=[
            pl.BlockSpec((tm, d), lambda i, j: (i, 0)),
            pl.BlockSpec((None, 1, d), lambda i, j: (layer, 0, 0)),
            pl.BlockSpec((None, d, tn), lambda i, j: (layer, 0, j)),
            pl.BlockSpec((None, d, LANES), lambda i, j: (layer, 0, 0)),
            rows(sc_width, sc_col), rows(sc_width, sc_col + 1), rows(sc_width, sc_col + 2),
            prev(sc_halo, sc_width, sc_col + 1), prev(sc_halo, sc_width, sc_col + 2),
            pl.BlockSpec((None, SC_KERNEL, sc_width), lambda i, j: (layer, 0, 0)), vec(sc_width),
            rows(cf_width, cf_col), rows(cf_width, cf_col + 1),
            prev(cf_halo, cf_width, cf_col), prev(cf_halo, cf_width, cf_col + 1),
            pl.BlockSpec((None, CF_KERNEL, CONV_ROWS, cf_width), lambda i, j: (layer, 0, 0, 0)),
            vec(cf_width), vec(cf_width), vec(cf_width),
        ],
        out_specs=[
            pl.BlockSpec((tm, tn), lambda i, j: (i, j)),
            pl.BlockSpec((tm, LANES), lambda i, j: (i, 0)),
            pl.BlockSpec((rs, sc_width), lambda i, j: (i * nb + j, 0)),
            pl.BlockSpec((rs, cf_width), lambda i, j: (i * nb + j, 0)),
        ],
        out_shape=[jax.ShapeDtypeStruct((m, n), BF16), jax.ShapeDtypeStruct((m, LANES), BF16),
                   jax.ShapeDtypeStruct((m, sc_width), BF16), jax.ShapeDtypeStruct((m, cf_width), BF16)],
        scratch_shapes=[pltpu.VMEM((tm, d), BF16), pltpu.VMEM((rs + sc_halo, sc_width), F32),
                        pltpu.VMEM((F32_SUBLANES, rs + cf_halo, cf_width), F32)],
        compiler_params=_params("parallel", "arbitrary"),
        name="gate_proj_and_convs",
    )(x, g, w_gate, w_lr, z, z, z, z, z, sc_w, sc_b, z, z, z, z, cf_w_rows, cf_b, ln_g, ln_b)


def _chunk_cumsum(gk, chunk):
    tt = gk.shape[0]
    row = lax.broadcasted_iota(jnp.int32, (tt, tt), 0)
    col = lax.broadcasted_iota(jnp.int32, (tt, tt), 1)
    shift = chunk.bit_length() - 1
    tril = jnp.where((col <= row) & ((row >> shift) == (col >> shift)), 1.0, 0.0).astype(BF16)
    g_hi = gk.astype(BF16)
    g_lo = (gk - g_hi.astype(F32)).astype(BF16)
    return _dot(tril, g_hi) + _dot(tril, g_lo)


def _advance_state(state_scr, b_last, k_end, vc):
    dk, dv = state_scr.shape
    upd = lax.dot_general(k_end, vc, (((0,), (0,)), ((), ())), preferred_element_type=F32)
    dec_col = jnp.transpose(jnp.broadcast_to(jnp.exp(b_last), (LANES, dk)))
    for l0 in range(0, dv, LANES):
        state_scr[:, l0:l0 + LANES] = state_scr[:, l0:l0 + LANES] * dec_col + upd[:, l0:l0 + LANES]


def _gla_factorised(qs, ks, bs, v_refs, state_scr, o_scr):
    tt = qs[0].shape[0]
    row = lax.broadcasted_iota(jnp.int32, (GLA_FAST_CHUNK, GLA_FAST_CHUNK), 0)
    col = lax.broadcasted_iota(jnp.int32, (GLA_FAST_CHUNK, GLA_FAST_CHUNK), 1)
    for base in range(0, tt, GLA_FAST_CHUNK):
        rows = slice(base, base + GLA_FAST_CHUNK)
        for h, (q, k, b, v_ref) in enumerate(zip(qs, ks, bs, v_refs)):
            bc, kc, vc = b[rows], k[rows], v_ref[rows, :]
            q_dec = (q[rows] * jnp.exp(bc)).astype(BF16)
            k_inv = (kc * jnp.exp(-bc)).astype(BF16)
            scores = lax.dot_general(q_dec, k_inv, (((1,), (1,)), ((), ())), preferred_element_type=F32)
            scores = jnp.where(col <= row, scores, 0.0).astype(BF16)
            o_scr[h, rows, :] = _dot(q_dec, state_scr[h].astype(BF16)) + _dot(scores, vc)
            b_last = bc[GLA_FAST_CHUNK - 1:GLA_FAST_CHUNK]
            k_end = (kc * jnp.exp(b_last - bc)).astype(BF16)
            _advance_state(state_scr.at[h], b_last, k_end, vc)


def _gla_nonpositive(q, k, gk, v_ref, state_scr, o_scr, b_scr, k_scr, q_scr):
    tt, dk = q.shape
    n_sub = GLA_CHUNK // GLA_SUB
    b_scr[...] = _chunk_cumsum(gk, GLA_CHUNK)
    k_scr[...] = k
    q_scr[...] = q
    rows_sub = lax.broadcasted_iota(jnp.int32, (GLA_SUB, dk), 0)
    rows_chunk = lax.broadcasted_iota(jnp.int32, (GLA_CHUNK, dk), 0)
    lane_chunk = lax.broadcasted_iota(jnp.int32, (GLA_SUB, GLA_CHUNK), 1)

    for base in range(0, tt, GLA_CHUNK):
        bc = b_scr[base:base + GLA_CHUNK, :]
        kc = k_scr[base:base + GLA_CHUNK, :]
        qc = q_scr[base:base + GLA_CHUNK, :]
        vc = v_ref[base:base + GLA_CHUNK, :]

        score_rows = []
        for blk in range(n_sub):
            r0 = base + blk * GLA_SUB
            q_blk = q_scr[r0:r0 + GLA_SUB, :]
            b_blk = b_scr[r0:r0 + GLA_SUB, :]

            def diag_step(j, s_acc, r0=r0, q_blk=q_blk, b_blk=b_blk, blk=blk):
                bj = b_scr[pl.ds(r0 + j, 1), :]
                kj = k_scr[pl.ds(r0 + j, 1), :]
                decay = jnp.where(rows_sub >= j, jnp.exp(jnp.minimum(b_blk - bj, 0.0)), 0.0)
                s = jnp.sum(q_blk * decay * kj, axis=-1, keepdims=True)
                return jnp.where(lane_chunk == blk * GLA_SUB + j, s, s_acc)

            s_blk = lax.fori_loop(0, GLA_SUB, diag_step, jnp.zeros((GLA_SUB, GLA_CHUNK), F32))
            if blk > 0:
                b_start = b_scr[r0 - 1:r0, :]
                k_dec = jnp.where(rows_chunk < blk * GLA_SUB,
                                  jnp.exp(jnp.minimum(b_start - bc, 0.0)), 0.0)
                k_sc = (kc * k_dec).astype(BF16)
                q_sc = (q_blk * jnp.exp(jnp.minimum(b_blk - b_start, 0.0))).astype(BF16)
                s_blk = s_blk + lax.dot_general(q_sc, k_sc, (((1,), (1,)), ((), ())),
                                                preferred_element_type=F32)
            score_rows.append(s_blk)
        scores = jnp.concatenate(score_rows, axis=0).astype(BF16)

        q_dec = (qc * jnp.exp(bc)).astype(BF16)
        o_scr[base:base + GLA_CHUNK, :] = _dot(q_dec, state_scr[...].astype(BF16)) + _dot(scores, vc)
        b_last = b_scr[base + GLA_CHUNK - 1:base + GLA_CHUNK, :]
        k_end = (kc * jnp.exp(jnp.minimum(b_last - bc, 0.0))).astype(BF16)
        _advance_state(state_scr, b_last, k_end, vc)


def _gla_kernel(*refs, heads):
    q_refs, k_refs, v_refs, go_refs = (refs[i * heads:(i + 1) * heads] for i in range(4))
    lr_ref, wgk_ref, bgk_ref, gn_ref, o_ref, state_scr, o_scr, b_scr, k_scr, q_scr = refs[4 * heads:]
    dk = q_refs[0].shape[1]
    dv = v_refs[0].shape[1]

    @pl.when(pl.program_id(2) == 0)
    def _():
        state_scr[...] = jnp.zeros_like(state_scr)

    x = _dot(lr_ref[...], wgk_ref[...]) + bgk_ref[...]
    gk = (jnp.minimum(x, 0.0) - jnp.log1p(jnp.exp(-jnp.abs(x)))) * (1.0 / GLA_GATE_NORMALIZER)
    b = _chunk_cumsum(gk, GLA_FAST_CHUNK)
    safe = jnp.max(-b) <= GLA_SAFE_DECAY
    qs = [r[...].astype(F32) * (dk ** -0.5) for r in q_refs]
    ks = [r[...].astype(F32) for r in k_refs]
    head = lambda a, h: a[:, h * dk:(h + 1) * dk]

    @pl.when(safe)
    def _():
        _gla_factorised(qs, ks, [head(b, h) for h in range(heads)], v_refs, state_scr, o_scr)

    @pl.when(jnp.logical_not(safe))
    def _():
        for h in range(heads):
            _gla_nonpositive(qs[h], ks[h], head(gk, h), v_refs[h], state_scr.at[h], o_scr.at[h],
                             b_scr, k_scr, q_scr)

    for h in range(heads):
        o = o_scr[h]
        g = go_refs[h][...].astype(F32)
        o_ref[:, h * dv:(h + 1) * dv] = (o * _rms_scale(o) * gn_ref[...] * (g * _sigmoid(g))).astype(o_ref.dtype)


def gla_mixer(z, z_lr, w_gk, b_gk, g_norm, layer, *, batch, seq, dk, dv, q_col, k_col, v_col, go_col,
              lr_col, tt, heads):
    m = z.shape[0]
    nt = seq // tt
    rows = lambda b, t: b * nt + t
    per_head = lambda width, col0: [
        pl.BlockSpec((tt, width), lambda b, hg, t, h=h: (rows(b, t), col0 + hg * heads + h)) for h in range(heads)]
    return pl.pallas_call(
        functools.partial(_gla_kernel, heads=heads),
        grid=(batch, GLA_HEADS // heads, nt),
        in_specs=per_head(dk, q_col) + per_head(dk, k_col) + per_head(dv, v_col) + per_head(dv, go_col) + [
            pl.BlockSpec((tt, LANES), lambda b, hg, t: (rows(b, t), lr_col)),
            pl.BlockSpec((None, LANES, heads * dk), lambda b, hg, t: (layer, 0, hg)),
            pl.BlockSpec((None, 1, heads * dk), lambda b, hg, t: (layer, 0, hg)),
            pl.BlockSpec((None, 1, dv), lambda b, hg, t: (layer, 0, 0)),
        ],
        out_specs=pl.BlockSpec((tt, heads * dv), lambda b, hg, t: (rows(b, t), hg)),
        out_shape=jax.ShapeDtypeStruct((m, GLA_HEADS * dv), BF16),
        scratch_shapes=[pltpu.VMEM((heads, dk, dv), F32), pltpu.VMEM((heads, tt, dv), F32),
                        pltpu.VMEM((tt, dk), F32), pltpu.VMEM((tt, dk), F32), pltpu.VMEM((tt, dk), F32)],
        compiler_params=_params("parallel", "parallel", "arbitrary"),
        name="gla_mixer",
    )(*([z] * (4 * heads)), z_lr, w_gk, b_gk, g_norm)


def _merge_kernel(a1_ref, a2_ref, a3_ref, m1_ref, m2_ref, m3_ref, w1_ref, w2_ref, w3_ref, o_ref):
    acc = _sigmoid(m1_ref[...].astype(F32)) * _wdot(a1_ref[...], w1_ref)
    acc = acc + _sigmoid(m2_ref[...].astype(F32)) * _wdot(a2_ref[...], w2_ref)
    acc = acc + _sigmoid(m3_ref[...].astype(F32)) * _wdot(a3_ref[...], w3_ref)
    o_ref[...] = acc.astype(o_ref.dtype)


def merge_mixers(a1, a2, a3, z, w1, w2, w3, layer, *, m_col, tm, tn):
    m = a1.shape[0]
    n = w1.shape[-1]
    nb = n // tn
    act = lambda a: pl.BlockSpec((tm, a.shape[1]), lambda i, j: (i, 0))
    gate = lambda c: pl.BlockSpec((tm, tn), lambda i, j: (i, m_col + c * nb + j))
    wgt = lambda w: pl.BlockSpec((None, w.shape[1], tn), lambda i, j: (layer, 0, j))
    return pl.pallas_call(
        _merge_kernel,
        grid=(m // tm, nb),
        in_specs=[act(a1), act(a2), act(a3), gate(0), gate(1), gate(2), wgt(w1), wgt(w2), wgt(w3)],
        out_specs=pl.BlockSpec((tm, tn), lambda i, j: (i, j)),
        out_shape=jax.ShapeDtypeStruct((m, n), BF16),
        compiler_params=_params("parallel", "parallel"),
        name="merge_mixers",
    )(a1, a2, a3, z, z, z, w1, w2, w3)


def _matmul_residual_kernel(a_ref, w_ref, x_ref, o_ref):
    o_ref[...] = x_ref[...] + _wdot(a_ref[...], w_ref)


def matmul_residual(a, w, x, layer, *, tm, tn):
    m, k = a.shape
    n = w.shape[-1]
    return pl.pallas_call(
        _matmul_residual_kernel,
        grid=(m // tm, n // tn),
        in_specs=[pl.BlockSpec((tm, k), lambda i, j: (i, 0)),
                  pl.BlockSpec((None, k, tn), lambda i, j: (layer, 0, j)),
                  pl.BlockSpec((tm, tn), lambda i, j: (i, j))],
        out_specs=pl.BlockSpec((tm, tn), lambda i, j: (i, j)),
        out_shape=jax.ShapeDtypeStruct((m, n), F32),
        compiler_params=_params("parallel", "parallel"),
        name="matmul_residual",
    )(a, w, x)


def _gate_up_kernel(x_ref, g_ref, wg_ref, wu_ref, o_ref, h_scr):
    @pl.when(pl.program_id(1) == 0)
    def _():
        x = x_ref[...]
        h_scr[...] = (x * _rms_scale(x) * g_ref[...]).astype(BF16)

    h = h_scr[...]
    gate = _wdot(h, wg_ref)
    up = _wdot(h, wu_ref)
    o_ref[...] = (gate * _sigmoid(gate) * up).astype(o_ref.dtype)


def gate_up(x, g, w, layer, *, tm, tn):
    m, d = x.shape
    f = w.shape[-1] // 2
    nb = f // tn
    return pl.pallas_call(
        _gate_up_kernel,
        grid=(m // tm, nb),
        in_specs=[pl.BlockSpec((tm, d), lambda i, j: (i, 0)),
                  pl.BlockSpec((None, 1, d), lambda i, j: (layer, 0, 0)),
                  pl.BlockSpec((None, d, tn), lambda i, j: (layer, 0, j)),
                  pl.BlockSpec((None, d, tn), lambda i, j: (layer, 0, nb + j))],
        out_specs=pl.BlockSpec((tm, tn), lambda i, j: (i, j)),
        out_shape=jax.ShapeDtypeStruct((m, f), BF16),
        scratch_shapes=[pltpu.VMEM((tm, d), BF16)],
        compiler_params=_params("parallel", "arbitrary"),
        name="gate_up",
    )(x, g, w, w)


def _ple_kernel(x_ref, g_ref, p_ref, wp_ref, wg_ref, o_ref, h_scr, p_scr):
    tn = o_ref.shape[1]

    @pl.when(pl.program_id(1) == 0)
    def _():
        x = x_ref[...]
        h_scr[...] = (x * _rms_scale(x) * g_ref[...]).astype(BF16)
        p_scr[...] = p_ref[...].astype(BF16)

    j0 = pl.multiple_of(pl.program_id(1) * tn, tn)
    emb = _wdot(p_scr[...], wp_ref)
    gate = _sigmoid(_wdot(h_scr[...], wg_ref))
    o_ref[...] = x_ref[:, pl.ds(j0, tn)] + emb * gate


def ple(x, g, p, wp, wg, layer, *, tm, tn):
    m, d = x.shape
    e = p.shape[-1]
    return pl.pallas_call(
        _ple_kernel,
        grid=(m // tm, d // tn),
        in_specs=[pl.BlockSpec((tm, d), lambda i, j: (i, 0)),
                  pl.BlockSpec((None, 1, d), lambda i, j: (layer, 0, 0)),
                  pl.BlockSpec((None, tm, e), lambda i, j: (layer, i, 0)),
                  pl.BlockSpec((None, e, tn), lambda i, j: (layer, 0, j)),
                  pl.BlockSpec((None, d, tn), lambda i, j: (layer, 0, j))],
        out_specs=pl.BlockSpec((tm, tn), lambda i, j: (i, j)),
        out_shape=jax.ShapeDtypeStruct((m, d), F32),
        scratch_shapes=[pltpu.VMEM((tm, d), BF16), pltpu.VMEM((tm, e), BF16)],
        compiler_params=_params("parallel", "arbitrary"),
        name="ple",
    )(x, g, p, wp, wg)


def _rmsnorm_kernel(x_ref, g_ref, o_ref):
    x = x_ref[...]
    o_ref[...] = x * _rms_scale(x) * g_ref[...]


def rmsnorm(x, g, *, tm):
    m, d = x.shape
    return pl.pallas_call(
        _rmsnorm_kernel,
        grid=(m // tm,),
        in_specs=[pl.BlockSpec((tm, d), lambda i: (i, 0)), pl.BlockSpec((1, d), lambda i: (0, 0))],
        out_specs=pl.BlockSpec((tm, d), lambda i: (i, 0)),
        out_shape=jax.ShapeDtypeStruct((m, d), F32),
        compiler_params=_params("parallel"),
        name="final_rmsnorm",
    )(x, g)


def kernel(x, p, g_mix, w_in, sc_conv_w, sc_conv_b, w_sc_out, cf_conv_w, cf_conv_b, cf_ln_g, cf_ln_b,
           w_cf_out, w_gla_gk, b_gla_gk, g_gla_norm, w_gla_out, w_o, g_ffn, w_gate_up, w_down, g_ple,
           w_ple_gate, w_ple, g_final):
    batch, seq, d = x.shape
    depth = p.shape[0]
    m = batch * seq
    sc_w = sc_conv_w.shape[-1]
    cf_w = cf_conv_w.shape[-1]
    key_dim = w_gla_gk.shape[-1]
    val_dim = w_gla_out.shape[1]
    dk, dv = key_dim // GLA_HEADS, val_dim // GLA_HEADS
    rank = w_gla_gk.shape[1]

    o_sc, o_cf = 0, 3 * sc_w
    o_q = o_cf + 2 * cf_w
    o_k = o_q + key_dim
    o_v = o_k + key_dim
    o_go = o_v + val_dim
    o_lr = o_go + val_dim
    o_m = o_lr + rank

    w_gate = w_in[:, :, o_m:].astype(BF16)
    w_lr = jnp.pad(w_in[:, :, o_lr:o_m], ((0, 0), (0, 0), (0, LANES - rank))).astype(BF16)
    w_gk = jnp.pad(w_gla_gk, ((0, 0), (0, LANES - rank), (0, 0))).astype(BF16)
    vec = lambda a: a.reshape(a.shape[0], 1, a.shape[1])
    g_mix, g_ffn, g_ple = vec(g_mix), vec(g_ffn), vec(g_ple)
    sc_conv_b, cf_conv_b, cf_ln_g, cf_ln_b = vec(sc_conv_b), vec(cf_conv_b), vec(cf_ln_g), vec(cf_ln_b)
    b_gla_gk, g_gla_norm = vec(b_gla_gk), vec(g_gla_norm)

    tm_big = min(1024, m)
    x = x.reshape(m, d)
    p = p.reshape(depth, m, p.shape[-1])
    for i in range(depth):
        z = norm_matmul(x, g_mix, w_in, i, n=o_lr, tm=tm_big, tn=_tile(o_lr, 1024))
        z_gate, z_lr, a_sc, a_cf = gate_proj_and_convs(
            x, g_mix, w_gate, w_lr, z, sc_conv_w, sc_conv_b, cf_conv_w, cf_conv_b, cf_ln_g, cf_ln_b, i,
            seq=seq, sc_width=sc_w, sc_col=o_sc // sc_w, cf_width=cf_w, cf_col=o_cf // cf_w,
            tm=tm_big, tn=_tile(3 * d, 768))
        a_gla = gla_mixer(z, z_lr, w_gk, b_gla_gk, g_gla_norm, i, batch=batch, seq=seq, dk=dk, dv=dv,
                          q_col=o_q // dk, k_col=o_k // dk, v_col=o_v // dv, go_col=o_go // dv,
                          lr_col=0, tt=min(512, seq), heads=GLA_HEADS)
        merged = merge_mixers(a_sc, a_cf, a_gla, z_gate, w_sc_out, w_cf_out, w_gla_out, i,
                              m_col=0, tm=tm_big, tn=_tile(d, 512))
        x = matmul_residual(merged, w_o, x, i, tm=tm_big, tn=_tile(d, 1024))
        act = gate_up(x, g_ffn, w_gate_up, i, tm=tm_big, tn=_tile(w_down.shape[1], 512))
        x = matmul_residual(act, w_down, x, i, tm=tm_big, tn=_tile(d, 256))
        x = ple(x, g_ple, p, w_ple, w_ple_gate, i, tm=tm_big, tn=_tile(d, 512))
    out = rmsnorm(x, g_final.reshape(1, d), tm=min(512, m))
    return out.reshape(batch, seq, d)
```

```python
import functools

import jax
import jax.numpy as jnp
from jax import lax
from jax.experimental import pallas as pl
from jax.experimental.pallas import tpu as pltpu

F32 = jnp.float32
BF16 = jnp.bfloat16

EPS = 1e-6
SC_KERNEL = 3
CF_KERNEL = 31
GLA_HEADS = 4
GLA_GATE_RANK = 16
GLA_GATE_NORMALIZER = 16.0
GLA_FAST_CHUNK = 128
GLA_SAFE_DECAY = 60.0
GLA_CHUNK = 64
GLA_SUB = 16
LANES = 128
F32_SUBLANES = 8
BF16_SUBLANES = 16
CONV_ROWS = 8
VMEM_LIMIT_BYTES = 56 * 1024 * 1024


def _tile(n, want):
    t = min(want, n) // LANES * LANES
    while n % t:
        t -= LANES
    return t


def _params(*sem):
    return pltpu.CompilerParams(dimension_semantics=sem, vmem_limit_bytes=VMEM_LIMIT_BYTES)


def _rms_scale(x):
    return lax.rsqrt(jnp.mean(x * x, axis=-1, keepdims=True) + EPS)


def _sigmoid(x):
    return 1.0 / (1.0 + jnp.exp(-x))


def _dot(a, b):
    return jnp.dot(a, b, preferred_element_type=F32)


def _wdot(a, w_ref):
    return _dot(a, w_ref[...].astype(BF16))


def _norm_matmul_kernel(x_ref, g_ref, w_ref, o_ref, h_scr):
    @pl.when(pl.program_id(1) == 0)
    def _():
        x = x_ref[...]
        h_scr[...] = (x * _rms_scale(x) * g_ref[...]).astype(BF16)

    o_ref[...] = _wdot(h_scr[...], w_ref).astype(o_ref.dtype)


def norm_matmul(x, g, w, layer, *, n, tm, tn):
    m, d = x.shape
    return pl.pallas_call(
        _norm_matmul_kernel,
        grid=(m // tm, n // tn),
        in_specs=[
            pl.BlockSpec((tm, d), lambda i, j: (i, 0)),
            pl.BlockSpec((None, 1, d), lambda i, j: (layer, 0, 0)),
            pl.BlockSpec((None, d, tn), lambda i, j: (layer, 0, j)),
        ],
        out_specs=pl.BlockSpec((tm, tn), lambda i, j: (i, j)),
        out_shape=jax.ShapeDtypeStruct((m, n), BF16),
        scratch_shapes=[pltpu.VMEM((tm, d), BF16)],
        compiler_params=_params("parallel", "arbitrary"),
        name="norm_matmul",
    )(x, g, w)


def _sc_rows(first, b_ref, c_ref, x_ref, ch_ref, xh_ref, w_ref, bias_ref, o_ref, y_scr):
    halo = ch_ref.shape[0]
    tm = c_ref.shape[0]
    yh = ch_ref[...].astype(F32) * xh_ref[...].astype(F32)
    y_scr[0:halo, :] = jnp.where(first, 0.0, yh)
    y = c_ref[...].astype(F32) * x_ref[...].astype(F32)
    y_scr[halo:halo + tm, :] = y
    u = bias_ref[...] + w_ref[SC_KERNEL - 1:SC_KERNEL, :] * y
    for k in range(SC_KERNEL - 1):
        off = halo - (SC_KERNEL - 1) + k
        u = u + w_ref[k:k + 1, :] * y_scr[off:off + tm, :]
    o_ref[...] = (b_ref[...].astype(F32) * u).astype(o_ref.dtype)


def _cf_prepare(first, a_ref, g_ref, ah_ref, gh_ref, y_scr):
    halo = ah_ref.shape[0]
    tm = a_ref.shape[0]
    yh = ah_ref[...].astype(F32) * _sigmoid(gh_ref[...].astype(F32))
    y_scr[0, 0:halo, :] = jnp.where(first, 0.0, yh)
    y_scr[0, halo:halo + tm, :] = a_ref[...].astype(F32) * _sigmoid(g_ref[...].astype(F32))
    span = tm + halo - F32_SUBLANES
    for s in range(1, F32_SUBLANES):
        y_scr[s, 0:span, :] = y_scr[0, s:s + span, :]


def _cf_conv_rows(r_lo, r_hi, w_ref, bias_ref, lng_ref, lnb_ref, o_ref, y_scr):
    halo = y_scr.shape[1] - o_ref.shape[0]
    rows = w_ref.shape[1]
    carry = jnp.zeros((rows, bias_ref.shape[1]), F32)
    for r0 in range(r_lo, r_hi, rows):
        acc = bias_ref[...] + carry
        for k in range(CF_KERNEL):
            off = halo - (CF_KERNEL - 1) + k + r0
            s, base = off % F32_SUBLANES, off // F32_SUBLANES * F32_SUBLANES
            acc = acc + w_ref[k] * y_scr[s, base:base + rows, :]
        mu = jnp.mean(acc, axis=-1, keepdims=True)
        cen = acc - mu
        var = jnp.mean(cen * cen, axis=-1, keepdims=True)
        u = cen * lax.rsqrt(var + EPS) * lng_ref[...] + lnb_ref[...]
        o_ref[r0:r0 + rows, :] = (u * _sigmoid(u)).astype(o_ref.dtype)
        carry = 0.0 * acc


def _gate_conv_kernel(x_ref, g_ref, w_ref, wlr_ref,
                      scb_ref, scc_ref, scx_ref, scch_ref, scxh_ref, scw_ref, scbias_ref,
                      cfa_ref, cfg_ref, cfah_ref, cfgh_ref, cfw_ref, cfbias_ref, lng_ref, lnb_ref,
                      o_ref, lr_ref, asc_ref, acf_ref, h_scr, sc_scr, cf_scr, *, slices_per_seq):
    i, j = pl.program_id(0), pl.program_id(1)

    @pl.when(j == 0)
    def _():
        x = x_ref[...]
        h_scr[...] = (x * _rms_scale(x) * g_ref[...]).astype(BF16)
        lr_ref[...] = _dot(h_scr[...], wlr_ref[...]).astype(lr_ref.dtype)

    o_ref[...] = _dot(h_scr[...], w_ref[...]).astype(o_ref.dtype)
    first = (i * pl.num_programs(1) + j) % slices_per_seq == 0
    _sc_rows(first, scb_ref, scc_ref, scx_ref, scch_ref, scxh_ref, scw_ref, scbias_ref, asc_ref, sc_scr)
    _cf_prepare(first, cfa_ref, cfg_ref, cfah_ref, cfgh_ref, cf_scr)
    _cf_conv_rows(0, acf_ref.shape[0], cfw_ref, cfbias_ref, lng_ref, lnb_ref, acf_ref, cf_scr)


def gate_proj_and_convs(x, g, w_gate, w_lr, z, sc_w, sc_b, cf_w, cf_b, ln_g, ln_b, layer,
                        *, seq, sc_width, sc_col, cf_width, cf_col, tm, tn):
    m, d = x.shape
    n = w_gate.shape[-1]
    nb = n // tn
    rs = tm // nb
    sc_halo, cf_halo = BF16_SUBLANES, 2 * BF16_SUBLANES
    assert tm % nb == 0 and rs % cf_halo == 0 and seq % rs == 0 and cf_halo >= CF_KERNEL - 1
    rows = lambda wd, col: pl.BlockSpec((rs, wd), lambda i, j: (i * nb + j, col))
    prev = lambda halo, wd, col: pl.BlockSpec(
        (halo, wd), lambda i, j: (jnp.maximum((i * nb + j) * (rs // halo) - 1, 0), col))
    vec = lambda wd: pl.BlockSpec((None, 1, wd), lambda i, j: (layer, 0, 0))
    cf_w_rows = jnp.broadcast_to(cf_w[:, :, None, :], cf_w.shape[:2] + (CONV_ROWS, cf_width))
    return pl.pallas_call(
        functools.partial(_gate_conv_kernel, slices_per_seq=seq // rs),
        grid=(m // tm, nb),
        in_specs=[
            pl.BlockSpec((tm, d), lambda i, j: (i, 0)),
            pl.BlockSpec((None, 1, d), lambda i, j: (layer, 0, 0)),
            pl.BlockSpec((None, d, tn), lambda i, j: (layer, 0, j)),
            pl.BlockSpec((None, d, LANES), lambda i, j: (layer, 0, 0)),
            rows(sc_width, sc_col), rows(sc_width, sc_col + 1), rows(sc_width, sc_col + 2),
            prev(sc_halo, sc_width, sc_col + 1), prev(sc_halo, sc_width, sc_col + 2),
            pl.BlockSpec((None, SC_KERNEL, sc_width), lambda i, j: (layer, 0, 0)), vec(sc_width),
            rows(cf_width, cf_col), rows(cf_width, cf_col + 1),
            prev(cf_halo, cf_width, cf_col), prev(cf_halo, cf_width, cf_col + 1),
            pl.BlockSpec((None, CF_KERNEL, CONV_ROWS, cf_width), lambda i, j: (layer, 0, 0, 0)),
            vec(cf_width), vec(cf_width), vec(cf_width),
        ],
        out_specs=[
            pl.BlockSpec((tm, tn), lambda i, j: (i, j)),
            pl.BlockSpec((tm, LANES), lambda i, j: (i, 0)),
            pl.BlockSpec((rs, sc_width), lambda i, j: (i * nb + j, 0)),
            pl.BlockSpec((rs, cf_width), lambda i, j: (i * nb + j, 0)),
        ],
        out_shape=[jax.ShapeDtypeStruct((m, n), BF16), jax.ShapeDtypeStruct((m, LANES), BF16),
                   jax.ShapeDtypeStruct((m, sc_width), BF16), jax.ShapeDtypeStruct((m, cf_width), BF16)],
        scratch_shapes=[pltpu.VMEM((tm, d), BF16), pltpu.VMEM((rs + sc_halo, sc_width), F32),
                        pltpu.VMEM((F32_SUBLANES, rs + cf_halo, cf_width), F32)],
        compiler_params=_params("parallel", "arbitrary"),
        name="gate_proj_and_convs",
    )(x, g, w_gate, w_lr, z, z, z, z, z, sc_w, sc_b, z, z, z, z, cf_w_rows, cf_b, ln_g, ln_b)


def _chunk_cumsum(gk, chunk):
    row = lax.broadcasted_iota(jnp.int32, (chunk, chunk), 0)
    col = lax.broadcasted_iota(jnp.int32, (chunk, chunk), 1)
    tril = jnp.where(col <= row, 1.0, 0.0).astype(BF16)
    g_hi = gk.astype(BF16)
    g_lo = (gk - g_hi.astype(F32)).astype(BF16)
    sums = [_dot(tril, g_hi[r0:r0 + chunk]) + _dot(tril, g_lo[r0:r0 + chunk])
            for r0 in range(0, gk.shape[0], chunk)]
    return jnp.concatenate(sums, axis=0)


def _advance_state(state_scr, b_last, k_end, vc):
    dk, dv = state_scr.shape
    upd = lax.dot_general(k_end, vc, (((0,), (0,)), ((), ())), preferred_element_type=F32)
    dec_col = jnp.transpose(jnp.broadcast_to(jnp.exp(b_last), (LANES, dk)))
    for l0 in range(0, dv, LANES):
        state_scr[:, l0:l0 + LANES] = state_scr[:, l0:l0 + LANES] * dec_col + upd[:, l0:l0 + LANES]


def _gla_factorised(qs, ks, bs, v_refs, state_scr, o_scr):
    tt = qs[0].shape[0]
    row = lax.broadcasted_iota(jnp.int32, (GLA_FAST_CHUNK, GLA_FAST_CHUNK), 0)
    col = lax.broadcasted_iota(jnp.int32, (GLA_FAST_CHUNK, GLA_FAST_CHUNK), 1)
    for base in range(0, tt, GLA_FAST_CHUNK):
        rows = slice(base, base + GLA_FAST_CHUNK)
        for h, (q, k, b, v_ref) in enumerate(zip(qs, ks, bs, v_refs)):
            bc, kc, vc = b[rows], k[rows], v_ref[rows, :]
            q_dec = (q[rows] * jnp.exp(bc)).astype(BF16)
            k_inv = (kc * jnp.exp(-bc)).astype(BF16)
            scores = lax.dot_general(q_dec, k_inv, (((1,), (1,)), ((), ())), preferred_element_type=F32)
            scores = jnp.where(col <= row, scores, 0.0).astype(BF16)
            o_scr[h, rows, :] = _dot(q_dec, state_scr[h].astype(BF16)) + _dot(scores, vc)
            b_last = bc[GLA_FAST_CHUNK - 1:GLA_FAST_CHUNK]
            k_end = (kc * jnp.exp(b_last - bc)).astype(BF16)
            _advance_state(state_scr.at[h], b_last, k_end, vc)


def _gla_nonpositive(q, k, gk, v_ref, state_scr, o_scr, b_scr, k_scr, q_scr):
    tt, dk = q.shape
    n_sub = GLA_CHUNK // GLA_SUB
    b_scr[...] = _chunk_cumsum(gk, GLA_CHUNK)
    k_scr[...] = k
    q_scr[...] = q
    rows_sub = lax.broadcasted_iota(jnp.int32, (GLA_SUB, dk), 0)
    rows_chunk = lax.broadcasted_iota(jnp.int32, (GLA_CHUNK, dk), 0)
    lane_chunk = lax.broadcasted_iota(jnp.int32, (GLA_SUB, GLA_CHUNK), 1)

    for base in range(0, tt, GLA_CHUNK):
        bc = b_scr[base:base + GLA_CHUNK, :]
        kc = k_scr[base:base + GLA_CHUNK, :]
        qc = q_scr[base:base + GLA_CHUNK, :]
        vc = v_ref[base:base + GLA_CHUNK, :]

        score_rows = []
        for blk in range(n_sub):
            r0 = base + blk * GLA_SUB
            q_blk = q_scr[r0:r0 + GLA_SUB, :]
            b_blk = b_scr[r0:r0 + GLA_SUB, :]

            def diag_step(j, s_acc, r0=r0, q_blk=q_blk, b_blk=b_blk, blk=blk):
                bj = b_scr[pl.ds(r0 + j, 1), :]
                kj = k_scr[pl.ds(r0 + j, 1), :]
                decay = jnp.where(rows_sub >= j, jnp.exp(jnp.minimum(b_blk - bj, 0.0)), 0.0)
                s = jnp.sum(q_blk * decay * kj, axis=-1, keepdims=True)
                return jnp.where(lane_chunk == blk * GLA_SUB + j, s, s_acc)

            s_blk = lax.fori_loop(0, GLA_SUB, diag_step, jnp.zeros((GLA_SUB, GLA_CHUNK), F32))
            if blk > 0:
                b_start = b_scr[r0 - 1:r0, :]
                k_dec = jnp.where(rows_chunk < blk * GLA_SUB,
                                  jnp.exp(jnp.minimum(b_start - bc, 0.0)), 0.0)
                k_sc = (kc * k_dec).astype(BF16)
                q_sc = (q_blk * jnp.exp(jnp.minimum(b_blk - b_start, 0.0))).astype(BF16)
                s_blk = s_blk + lax.dot_general(q_sc, k_sc, (((1,), (1,)), ((), ())),
                                                preferred_element_type=F32)
            score_rows.append(s_blk)
        scores = jnp.concatenate(score_rows, axis=0).astype(BF16)

        q_dec = (qc * jnp.exp(bc)).astype(BF16)
        o_scr[base:base + GLA_CHUNK, :] = _dot(q_dec, state_scr[...].astype(BF16)) + _dot(scores, vc)
        b_last = b_scr[base + GLA_CHUNK - 1:base + GLA_CHUNK, :]
        k_end = (kc * jnp.exp(jnp.minimum(b_last - bc, 0.0))).astype(BF16)
        _advance_state(state_scr, b_last, k_end, vc)


def _gla_kernel(*refs, heads):
    q_refs, k_refs, v_refs, go_refs = (refs[i * heads:(i + 1) * heads] for i in range(4))
    lr_ref, wgk_ref, bgk_ref, gn_ref, o_ref, state_scr, o_scr, b_scr, k_scr, q_scr = refs[4 * heads:]
    dk = q_refs[0].shape[1]
    dv = v_refs[0].shape[1]

    @pl.when(pl.program_id(2) == 0)
    def _():
        state_scr[...] = jnp.zeros_like(state_scr)

    x = _dot(lr_ref[...], wgk_ref[...]) + bgk_ref[...]
    gk = (jnp.minimum(x, 0.0) - jnp.log1p(jnp.exp(-jnp.abs(x)))) * (1.0 / GLA_GATE_NORMALIZER)
    b = _chunk_cumsum(gk, GLA_FAST_CHUNK)
    safe = jnp.max(-b) <= GLA_SAFE_DECAY
    qs = [r[...].astype(F32) * (dk ** -0.5) for r in q_refs]
    ks = [r[...].astype(F32) for r in k_refs]
    head = lambda a, h: a[:, h * dk:(h + 1) * dk]

    @pl.when(safe)
    def _():
        _gla_factorised(qs, ks, [head(b, h) for h in range(heads)], v_refs, state_scr, o_scr)

    @pl.when(jnp.logical_not(safe))
    def _():
        for h in range(heads):
            _gla_nonpositive(qs[h], ks[h], head(gk, h), v_refs[h], state_scr.at[h], o_scr.at[h],
                             b_scr, k_scr, q_scr)

    for h in range(heads):
        o = o_scr[h]
        g = go_refs[h][...].astype(F32)
        o_ref[:, h * dv:(h + 1) * dv] = (o * _rms_scale(o) * gn_ref[...] * (g * _sigmoid(g))).astype(o_ref.dtype)


def gla_mixer(z, z_lr, w_gk, b_gk, g_norm, layer, *, batch, seq, dk, dv, q_col, k_col, v_col, go_col,
              lr_col, tt, heads):
    m = z.shape[0]
    nt = seq // tt
    rows = lambda b, t: b * nt + t
    per_head = lambda width, col0: [
        pl.BlockSpec((tt, width), lambda b, hg, t, h=h: (rows(b, t), col0 + hg * heads + h)) for h in range(heads)]
    return pl.pallas_call(
        functools.partial(_gla_kernel, heads=heads),
        grid=(batch, GLA_HEADS // heads, nt),
        in_specs=per_head(dk, q_col) + per_head(dk, k_col) + per_head(dv, v_col) + per_head(dv, go_col) + [
            pl.BlockSpec((tt, LANES), lambda b, hg, t: (rows(b, t), lr_col)),
            pl.BlockSpec((None, LANES, heads * dk), lambda b, hg, t: (layer, 0, hg)),
            pl.BlockSpec((None, 1, heads * dk), lambda b, hg, t: (layer, 0, hg)),
            pl.BlockSpec((None, 1, dv), lambda b, hg, t: (layer, 0, 0)),
        ],
        out_specs=pl.BlockSpec((tt, heads * dv), lambda b, hg, t: (rows(b, t), hg)),
        out_shape=jax.ShapeDtypeStruct((m, GLA_HEADS * dv), BF16),
        scratch_shapes=[pltpu.VMEM((heads, dk, dv), F32), pltpu.VMEM((heads, tt, dv), F32),
                        pltpu.VMEM((tt, dk), F32), pltpu.VMEM((tt, dk), F32), pltpu.VMEM((tt, dk), F32)],
        compiler_params=_params("parallel", "parallel", "arbitrary"),
        name="gla_mixer",
    )(*([z] * (4 * heads)), z_lr, w_gk, b_gk, g_norm)


def _merge_kernel(a1_ref, a2_ref, a3_ref, m1_ref, m2_ref, m3_ref, w1_ref, w2_ref, w3_ref, o_ref):
    acc = _sigmoid(m1_ref[...].astype(F32)) * _wdot(a1_ref[...], w1_ref)
    acc = acc + _sigmoid(m2_ref[...].astype(F32)) * _wdot(a2_ref[...], w2_ref)
    acc = acc + _sigmoid(m3_ref[...].astype(F32)) * _wdot(a3_ref[...], w3_ref)
    o_ref[...] = acc.astype(o_ref.dtype)


def merge_mixers(a1, a2, a3, z, w1, w2, w3, layer, *, m_col, tm, tn):
    m = a1.shape[0]
    n = w1.shape[-1]
    nb = n // tn
    act = lambda a: pl.BlockSpec((tm, a.shape[1]), lambda i, j: (i, 0))
    gate = lambda c: pl.BlockSpec((tm, tn), lambda i, j: (i, m_col + c * nb + j))
    wgt = lambda w: pl.BlockSpec((None, w.shape[1], tn), lambda i, j: (layer, 0, j))
    return pl.pallas_call(
        _merge_kernel,
        grid=(m // tm, nb),
        in_specs=[act(a1), act(a2), act(a3), gate(0), gate(1), gate(2), wgt(w1), wgt(w2), wgt(w3)],
        out_specs=pl.BlockSpec((tm, tn), lambda i, j: (i, j)),
        out_shape=jax.ShapeDtypeStruct((m, n), BF16),
        compiler_params=_params("parallel", "parallel"),
        name="merge_mixers",
    )(a1, a2, a3, z, z, z, w1, w2, w3)


def _matmul_residual_kernel(a_ref, w_ref, x_ref, o_ref):
    o_ref[...] = x_ref[...] + _wdot(a_ref[...], w_ref)


def matmul_residual(a, w, x, layer, *, tm, tn):
    m, k = a.shape
    n = w.shape[-1]
    return pl.pallas_call(
        _matmul_residual_kernel,
        grid=(m // tm, n // tn),
        in_specs=[pl.BlockSpec((tm, k), lambda i, j: (i, 0)),
                  pl.BlockSpec((None, k, tn), lambda i, j: (layer, 0, j)),
                  pl.BlockSpec((tm, tn), lambda i, j: (i, j))],
        out_specs=pl.BlockSpec((tm, tn), lambda i, j: (i, j)),
        out_shape=jax.ShapeDtypeStruct((m, n), F32),
        compiler_params=_params("parallel", "parallel"),
        name="matmul_residual",
    )(a, w, x)


def _gate_up_kernel(x_ref, g_ref, wg_ref, wu_ref, o_ref, h_scr):
    @pl.when(pl.program_id(1) == 0)
    def _():
        x = x_ref[...]
        h_scr[...] = (x * _rms_scale(x) * g_ref[...]).astype(BF16)

    h = h_scr[...]
    gate = _wdot(h, wg_ref)
    up = _wdot(h, wu_ref)
    o_ref[...] = (gate * _sigmoid(gate) * up).astype(o_ref.dtype)


def gate_up(x, g, w, layer, *, tm, tn):
    m, d = x.shape
    f = w.shape[-1] // 2
    nb = f // tn
    return pl.pallas_call(
        _gate_up_kernel,
        grid=(m // tm, nb),
        in_specs=[pl.BlockSpec((tm, d), lambda i, j: (i, 0)),
                  pl.BlockSpec((None, 1, d), lambda i, j: (layer, 0, 0)),
                  pl.BlockSpec((None, d, tn), lambda i, j: (layer, 0, j)),
                  pl.BlockSpec((None, d, tn), lambda i, j: (layer, 0, nb + j))],
        out_specs=pl.BlockSpec((tm, tn), lambda i, j: (i, j)),
        out_shape=jax.ShapeDtypeStruct((m, f), BF16),
        scratch_shapes=[pltpu.VMEM((tm, d), BF16)],
        compiler_params=_params("parallel", "arbitrary"),
        name="gate_up",
    )(x, g, w, w)


def _ple_kernel(x_ref, g_ref, p_ref, wp_ref, wg_ref, o_ref, h_scr, p_scr):
    tn = o_ref.shape[1]

    @pl.when(pl.program_id(1) == 0)
    def _():
        x = x_ref[...]
        h_scr[...] = (x * _rms_scale(x) * g_ref[...]).astype(BF16)
        p_scr[...] = p_ref[...].astype(BF16)

    j0 = pl.multiple_of(pl.program_id(1) * tn, tn)
    emb = _wdot(p_scr[...], wp_ref)
    gate = _sigmoid(_wdot(h_scr[...], wg_ref))
    o_ref[...] = x_ref[:, pl.ds(j0, tn)] + emb * gate


def ple(x, g, p, wp, wg, layer, *, tm, tn):
    m, d = x.shape
    e = p.shape[-1]
    return pl.pallas_call(
        _ple_kernel,
        grid=(m // tm, d // tn),
        in_specs=[pl.BlockSpec((tm, d), lambda i, j: (i, 0)),
                  pl.BlockSpec((None, 1, d), lambda i, j: (layer, 0, 0)),
                  pl.BlockSpec((None, tm, e), lambda i, j: (layer, i, 0)),
                  pl.BlockSpec((None, e, tn), lambda i, j: (layer, 0, j)),
                  pl.BlockSpec((None, d, tn), lambda i, j: (layer, 0, j))],
        out_specs=pl.BlockSpec((tm, tn), lambda i, j: (i, j)),
        out_shape=jax.ShapeDtypeStruct((m, d), F32),
        scratch_shapes=[pltpu.VMEM((tm, d), BF16), pltpu.VMEM((tm, e), BF16)],
        compiler_params=_params("parallel", "arbitrary"),
        name="ple",
    )(x, g, p, wp, wg)


def _rmsnorm_kernel(x_ref, g_ref, o_ref):
    x = x_ref[...]
    o_ref[...] = x * _rms_scale(x) * g_ref[...]


def rmsnorm(x, g, *, tm):
    m, d = x.shape
    return pl.pallas_call(
        _rmsnorm_kernel,
        grid=(m // tm,),
        in_specs=[pl.BlockSpec((tm, d), lambda i: (i, 0)), pl.BlockSpec((1, d), lambda i: (0, 0))],
        out_specs=pl.BlockSpec((tm, d), lambda i: (i, 0)),
        out_shape=jax.ShapeDtypeStruct((m, d), F32),
        compiler_params=_params("parallel"),
        name="final_rmsnorm",
    )(x, g)


def kernel(x, p, g_mix, w_in, sc_conv_w, sc_conv_b, w_sc_out, cf_conv_w, cf_conv_b, cf_ln_g, cf_ln_b,
           w_cf_out, w_gla_gk, b_gla_gk, g_gla_norm, w_gla_out, w_o, g_ffn, w_gate_up, w_down, g_ple,
           w_ple_gate, w_ple, g_final):
    batch, seq, d = x.shape
    depth = p.shape[0]
    m = batch * seq
    sc_w = sc_conv_w.shape[-1]
    cf_w = cf_conv_w.shape[-1]
    key_dim = w_gla_gk.shape[-1]
    val_dim = w_gla_out.shape[1]
    dk, dv = key_dim // GLA_HEADS, val_dim // GLA_HEADS
    rank = w_gla_gk.shape[1]

    o_sc, o_cf = 0, 3 * sc_w
    o_q = o_cf + 2 * cf_w
    o_k = o_q + key_dim
    o_v = o_k + key_dim
    o_go = o_v + val_dim
    o_lr = o_go + val_dim
    o_m = o_lr + rank

    w_in = w_in.astype(BF16)
    w_gate = w_in[:, :, o_m:]
    w_lr = jnp.pad(w_in[:, :, o_lr:o_m], ((0, 0), (0, 0), (0, LANES - rank)))
    w_sc_out, w_cf_out, w_gla_out, w_o = (a.astype(BF16) for a in (w_sc_out, w_cf_out, w_gla_out, w_o))
    w_down, w_ple_gate, w_ple = (a.astype(BF16) for a in (w_down, w_ple_gate, w_ple))
    w_gk = jnp.pad(w_gla_gk, ((0, 0), (0, LANES - rank), (0, 0))).astype(BF16)
    vec = lambda a: a.reshape(a.shape[0], 1, a.shape[1])
    g_mix, g_ffn, g_ple = vec(g_mix), vec(g_ffn), vec(g_ple)
    sc_conv_b, cf_conv_b, cf_ln_g, cf_ln_b = vec(sc_conv_b), vec(cf_conv_b), vec(cf_ln_g), vec(cf_ln_b)
    b_gla_gk, g_gla_norm = vec(b_gla_gk), vec(g_gla_norm)

    tm_big = min(1024, m)
    x = x.reshape(m, d)
    p = p.reshape(depth, m, p.shape[-1])
    for i in range(depth):
        z = norm_matmul(x, g_mix, w_in, i, n=o_lr, tm=tm_big, tn=_tile(o_lr, 1024))
        z_gate, z_lr, a_sc, a_cf = gate_proj_and_convs(
            x, g_mix, w_gate, w_lr, z, sc_conv_w, sc_conv_b, cf_conv_w, cf_conv_b, cf_ln_g, cf_ln_b, i,
            seq=seq, sc_width=sc_w, sc_col=o_sc // sc_w, cf_width=cf_w, cf_col=o_cf // cf_w,
            tm=tm_big, tn=_tile(3 * d, 768))
        a_gla = gla_mixer(z, z_lr, w_gk, b_gla_gk, g_gla_norm, i, batch=batch, seq=seq, dk=dk, dv=dv,
                          q_col=o_q // dk, k_col=o_k // dk, v_col=o_v // dv, go_col=o_go // dv,
                          lr_col=0, tt=min(512, seq), heads=GLA_HEADS)
        merged = merge_mixers(a_sc, a_cf, a_gla, z_gate, w_sc_out, w_cf_out, w_gla_out, i,
                              m_col=0, tm=tm_big, tn=_tile(d, 512))
        x = matmul_residual(merged, w_o, x, i, tm=tm_big, tn=_tile(d, 1024))
        act = gate_up(x, g_ffn, w_gate_up, i, tm=tm_big, tn=_tile(w_down.shape[1], 512))
        x = matmul_residual(act, w_down, x, i, tm=tm_big, tn=_tile(d, 512))
        x = ple(x, g_ple, p, w_ple, w_ple_gate, i, tm=tm_big, tn=_tile(d, 1024))
    out = rmsnorm(x, g_final.reshape(1, d), tm=min(512, m))
    return out.reshape(batch, seq, d)
```

```python
import functools

import jax
import jax.numpy as jnp
from jax import lax
from jax.experimental import pallas as pl
from jax.experimental.pallas import tpu as pltpu

F32 = jnp.float32
BF16 = jnp.bfloat16

EPS = 1e-6
SC_KERNEL = 3
CF_KERNEL = 31
GLA_HEADS = 4
GLA_GATE_RANK = 16
GLA_GATE_NORMALIZER = 16.0
GLA_FAST_CHUNK = 256
GLA_SAFE_DECAY = 60.0
GLA_CHUNK = 64
GLA_SUB = 16
LANES = 128
F32_SUBLANES = 8
BF16_SUBLANES = 16
CONV_ROWS = 8
VMEM_LIMIT_BYTES = 56 * 1024 * 1024


def _tile(n, want):
    t = min(want, n) // LANES * LANES
    while n % t:
        t -= LANES
    return t


def _params(*sem):
    return pltpu.CompilerParams(dimension_semantics=sem, vmem_limit_bytes=VMEM_LIMIT_BYTES)


def _rms_scale(x):
    return lax.rsqrt(jnp.mean(x * x, axis=-1, keepdims=True) + EPS)


def _sigmoid(x):
    return 1.0 / (1.0 + jnp.exp(-x))


def _dot(a, b):
    return jnp.dot(a, b, preferred_element_type=F32)


def _wdot(a, w_ref):
    return _dot(a, w_ref[...].astype(BF16))


def _norm_matmul_kernel(x_ref, g_ref, w_ref, o_ref, h_scr):
    @pl.when(pl.program_id(1) == 0)
    def _():
        x = x_ref[...]
        h_scr[...] = (x * _rms_scale(x) * g_ref[...]).astype(BF16)

    o_ref[...] = _wdot(h_scr[...], w_ref).astype(o_ref.dtype)


def norm_matmul(x, g, w, layer, *, n, tm, tn):
    m, d = x.shape
    return pl.pallas_call(
        _norm_matmul_kernel,
        grid=(m // tm, n // tn),
        in_specs=[
            pl.BlockSpec((tm, d), lambda i, j: (i, 0)),
            pl.BlockSpec((None, 1, d), lambda i, j: (layer, 0, 0)),
            pl.BlockSpec((None, d, tn), lambda i, j: (layer, 0, j)),
        ],
        out_specs=pl.BlockSpec((tm, tn), lambda i, j: (i, j)),
        out_shape=jax.ShapeDtypeStruct((m, n), BF16),
        scratch_shapes=[pltpu.VMEM((tm, d), BF16)],
        compiler_params=_params("parallel", "arbitrary"),
        name="norm_matmul",
    )(x, g, w)


def _sc_rows(first, b_ref, c_ref, x_ref, ch_ref, xh_ref, w_ref, bias_ref, o_ref, y_scr):
    halo = ch_ref.shape[0]
    tm = c_ref.shape[0]
    yh = ch_ref[...].astype(F32) * xh_ref[...].astype(F32)
    y_scr[0:halo, :] = jnp.where(first, 0.0, yh)
    y = c_ref[...].astype(F32) * x_ref[...].astype(F32)
    y_scr[halo:halo + tm, :] = y
    u = bias_ref[...] + w_ref[SC_KERNEL - 1:SC_KERNEL, :] * y
    for k in range(SC_KERNEL - 1):
        off = halo - (SC_KERNEL - 1) + k
        u = u + w_ref[k:k + 1, :] * y_scr[off:off + tm, :]
    o_ref[...] = (b_ref[...].astype(F32) * u).astype(o_ref.dtype)


def _cf_prepare(first, a_ref, g_ref, ah_ref, gh_ref, y_scr):
    halo = ah_ref.shape[0]
    tm = a_ref.shape[0]
    yh = ah_ref[...].astype(F32) * _sigmoid(gh_ref[...].astype(F32))
    y_scr[0, 0:halo, :] = jnp.where(first, 0.0, yh)
    y_scr[0, halo:halo + tm, :] = a_ref[...].astype(F32) * _sigmoid(g_ref[...].astype(F32))
    span = tm + halo - F32_SUBLANES
    for s in range(1, F32_SUBLANES):
        y_scr[s, 0:span, :] = y_scr[0, s:s + span, :]


def _cf_conv_rows(r_lo, r_hi, w_ref, bias_ref, lng_ref, lnb_ref, o_ref, y_scr):
    halo = y_scr.shape[1] - o_ref.shape[0]
    rows = w_ref.shape[1]
    carry = jnp.zeros((rows, bias_ref.shape[1]), F32)
    for r0 in range(r_lo, r_hi, rows):
        acc = bias_ref[...] + carry
        for k in range(CF_KERNEL):
            off = halo - (CF_KERNEL - 1) + k + r0
            s, base = off % F32_SUBLANES, off // F32_SUBLANES * F32_SUBLANES
            acc = acc + w_ref[k] * y_scr[s, base:base + rows, :]
        mu = jnp.mean(acc, axis=-1, keepdims=True)
        cen = acc - mu
        var = jnp.mean(cen * cen, axis=-1, keepdims=True)
        u = cen * lax.rsqrt(var + EPS) * lng_ref[...] + lnb_ref[...]
        o_ref[r0:r0 + rows, :] = (u * _sigmoid(u)).astype(o_ref.dtype)
        carry = 0.0 * acc


def _gate_conv_kernel(x_ref, g_ref, w_ref, wlr_ref,
                      scb_ref, scc_ref, scx_ref, scch_ref, scxh_ref, scw_ref, scbias_ref,
                      cfa_ref, cfg_ref, cfah_ref, cfgh_ref, cfw_ref, cfbias_ref, lng_ref, lnb_ref,
                      o_ref, lr_ref, asc_ref, acf_ref, h_scr, sc_scr, cf_scr, *, slices_per_seq):
    i, j = pl.program_id(0), pl.program_id(1)

    @pl.when(j == 0)
    def _():
        x = x_ref[...]
        h_scr[...] = (x * _rms_scale(x) * g_ref[...]).astype(BF16)
        lr_ref[...] = _dot(h_scr[...], wlr_ref[...]).astype(lr_ref.dtype)

    o_ref[...] = _dot(h_scr[...], w_ref[...]).astype(o_ref.dtype)
    first = (i * pl.num_programs(1) + j) % slices_per_seq == 0
    _sc_rows(first, scb_ref, scc_ref, scx_ref, scch_ref, scxh_ref, scw_ref, scbias_ref, asc_ref, sc_scr)
    _cf_prepare(first, cfa_ref, cfg_ref, cfah_ref, cfgh_ref, cf_scr)
    _cf_conv_rows(0, acf_ref.shape[0], cfw_ref, cfbias_ref, lng_ref, lnb_ref, acf_ref, cf_scr)


def gate_proj_and_convs(x, g, w_gate, w_lr, z, sc_w, sc_b, cf_w, cf_b, ln_g, ln_b, layer,
                        *, seq, sc_width, sc_col, cf_width, cf_col, tm, tn):
    m, d = x.shape
    n = w_gate.shape[-1]
    nb = n // tn
    rs = tm // nb
    sc_halo, cf_halo = BF16_SUBLANES, 2 * BF16_SUBLANES
    assert tm % nb == 0 and rs % cf_halo == 0 and seq % rs == 0 and cf_halo >= CF_KERNEL - 1
    rows = lambda wd, col: pl.BlockSpec((rs, wd), lambda i, j: (i * nb + j, col))
    prev = lambda halo, wd, col: pl.BlockSpec(
        (halo, wd), lambda i, j: (jnp.maximum((i * nb + j) * (rs // halo) - 1, 0), col))
    vec = lambda wd: pl.BlockSpec((None, 1, wd), lambda i, j: (layer, 0, 0))
    cf_w_rows = jnp.broadcast_to(cf_w[:, :, None, :], cf_w.shape[:2] + (CONV_ROWS, cf_width))
    return pl.pallas_call(
        functools.partial(_gate_conv_kernel, slices_per_seq=seq // rs),
        grid=(m // tm, nb),
        in_specs=[
            pl.BlockSpec((tm, d), lambda i, j: (i, 0)),
            pl.BlockSpec((None, 1, d), lambda i, j: (layer, 0, 0)),
            pl.BlockSpec((None, d, tn), lambda i, j: (layer, 0, j)),
            pl.BlockSpec((None, d, LANES), lambda i, j: (layer, 0, 0)),
            rows(sc_width, sc_col), rows(sc_width, sc_col + 1), rows(sc_width, sc_col + 2),
            prev(sc_halo, sc_width, sc_col + 1), prev(sc_halo, sc_width, sc_col + 2),
            pl.BlockSpec((None, SC_KERNEL, sc_width), lambda i, j: (layer, 0, 0)), vec(sc_width),
            rows(cf_width, cf_col), rows(cf_width, cf_col + 1),
            prev(cf_halo, cf_width, cf_col), prev(cf_halo, cf_width, cf_col + 1),
            pl.BlockSpec((None, CF_KERNEL, CONV_ROWS, cf_width), lambda i, j: (layer, 0, 0, 0)),
            vec(cf_width), vec(cf_width), vec(cf_width),
        ],
        out_specs=[
            pl.BlockSpec((tm, tn), lambda i, j: (i, j)),
            pl.BlockSpec((tm, LANES), lambda i, j: (i, 0)),
            pl.BlockSpec((rs, sc_width), lambda i, j: (i * nb + j, 0)),
            pl.BlockSpec((rs, cf_width), lambda i, j: (i * nb + j, 0)),
        ],
        out_shape=[jax.ShapeDtypeStruct((m, n), BF16), jax.ShapeDtypeStruct((m, LANES), BF16),
                   jax.ShapeDtypeStruct((m, sc_width), BF16), jax.ShapeDtypeStruct((m, cf_width), BF16)],
        scratch_shapes=[pltpu.VMEM((tm, d), BF16), pltpu.VMEM((rs + sc_halo, sc_width), F32),
                        pltpu.VMEM((F32_SUBLANES, rs + cf_halo, cf_width), F32)],
        compiler_params=_params("parallel", "arbitrary"),
        name="gate_proj_and_convs",
    )(x, g, w_gate, w_lr, z, z, z, z, z, sc_w, sc_b, z, z, z, z, cf_w_rows, cf_b, ln_g, ln_b)


def _chunk_cumsum(gk, chunk):
    row = lax.broadcasted_iota(jnp.int32, (chunk, chunk), 0)
    col = lax.broadcasted_iota(jnp.int32, (chunk, chunk), 1)
    tril = jnp.where(col <= row, 1.0, 0.0).astype(BF16)
    g_hi = gk.astype(BF16)
    g_lo = (gk - g_hi.astype(F32)).astype(BF16)
    sums = [_dot(tril, g_hi[r0:r0 + chunk]) + _dot(tril, g_lo[r0:r0 + chunk])
            for r0 in range(0, gk.shape[0], chunk)]
    return jnp.concatenate(sums, axis=0)


def _advance_state(state_scr, b_last, k_end, vc):
    dk, dv = state_scr.shape
    upd = lax.dot_general(k_end, vc, (((0,), (0,)), ((), ())), preferred_element_type=F32)
    dec_col = jnp.transpose(jnp.broadcast_to(jnp.exp(b_last), (LANES, dk)))
    for l0 in range(0, dv, LANES):
        state_scr[:, l0:l0 + LANES] = state_scr[:, l0:l0 + LANES] * dec_col + upd[:, l0:l0 + LANES]


def _gla_factorised(qs, ks, bs, v_refs, state_scr, o_scr):
    tt = qs[0].shape[0]
    row = lax.broadcasted_iota(jnp.int32, (GLA_FAST_CHUNK, GLA_FAST_CHUNK), 0)
    col = lax.broadcasted_iota(jnp.int32, (GLA_FAST_CHUNK, GLA_FAST_CHUNK), 1)
    for base in range(0, tt, GLA_FAST_CHUNK):
        rows = slice(base, base + GLA_FAST_CHUNK)
        for h, (q, k, b, v_ref) in enumerate(zip(qs, ks, bs, v_refs)):
            bc, kc, vc = b[rows], k[rows], v_ref[rows, :]
            q_dec = (q[rows] * jnp.exp(bc)).astype(BF16)
            k_inv = (kc * jnp.exp(-bc)).astype(BF16)
            scores = lax.dot_general(q_dec, k_inv, (((1,), (1,)), ((), ())), preferred_element_type=F32)
            scores = jnp.where(col <= row, scores, 0.0).astype(BF16)
            o_scr[h, rows, :] = _dot(q_dec, state_scr[h].astype(BF16)) + _dot(scores, vc)
            b_last = bc[GLA_FAST_CHUNK - 1:GLA_FAST_CHUNK]
            k_end = (kc * jnp.exp(b_last - bc)).astype(BF16)
            _advance_state(state_scr.at[h], b_last, k_end, vc)


def _gla_nonpositive(q, k, gk, v_ref, state_scr, o_scr, b_scr, k_scr, q_scr):
    tt, dk = q.shape
    n_sub = GLA_CHUNK // GLA_SUB
    b_scr[...] = _chunk_cumsum(gk, GLA_CHUNK)
    k_scr[...] = k
    q_scr[...] = q
    rows_sub = lax.broadcasted_iota(jnp.int32, (GLA_SUB, dk), 0)
    rows_chunk = lax.broadcasted_iota(jnp.int32, (GLA_CHUNK, dk), 0)
    lane_chunk = lax.broadcasted_iota(jnp.int32, (GLA_SUB, GLA_CHUNK), 1)

    for base in range(0, tt, GLA_CHUNK):
        bc = b_scr[base:base + GLA_CHUNK, :]
        kc = k_scr[base:base + GLA_CHUNK, :]
        qc = q_scr[base:base + GLA_CHUNK, :]
        vc = v_ref[base:base + GLA_CHUNK, :]

        score_rows = []
        for blk in range(n_sub):
            r0 = base + blk * GLA_SUB
            q_blk = q_scr[r0:r0 + GLA_SUB, :]
            b_blk = b_scr[r0:r0 + GLA_SUB, :]

            def diag_step(j, s_acc, r0=r0, q_blk=q_blk, b_blk=b_blk, blk=blk):
                bj = b_scr[pl.ds(r0 + j, 1), :]
                kj = k_scr[pl.ds(r0 + j, 1), :]
                decay = jnp.where(rows_sub >= j, jnp.exp(jnp.minimum(b_blk - bj, 0.0)), 0.0)
                s = jnp.sum(q_blk * decay * kj, axis=-1, keepdims=True)
                return jnp.where(lane_chunk == blk * GLA_SUB + j, s, s_acc)

            s_blk = lax.fori_loop(0, GLA_SUB, diag_step, jnp.zeros((GLA_SUB, GLA_CHUNK), F32))
            if blk > 0:
                b_start = b_scr[r0 - 1:r0, :]
                k_dec = jnp.where(rows_chunk < blk * GLA_SUB,
                                  jnp.exp(jnp.minimum(b_start - bc, 0.0)), 0.0)
                k_sc = (kc * k_dec).astype(BF16)
                q_sc = (q_blk * jnp.exp(jnp.minimum(b_blk - b_start, 0.0))).astype(BF16)
                s_blk = s_blk + lax.dot_general(q_sc, k_sc, (((1,), (1,)), ((), ())),
                                                preferred_element_type=F32)
            score_rows.append(s_blk)
        scores = jnp.concatenate(score_rows, axis=0).astype(BF16)

        q_dec = (qc * jnp.exp(bc)).astype(BF16)
        o_scr[base:base + GLA_CHUNK, :] = _dot(q_dec, state_scr[...].astype(BF16)) + _dot(scores, vc)
        b_last = b_scr[base + GLA_CHUNK - 1:base + GLA_CHUNK, :]
        k_end = (kc * jnp.exp(jnp.minimum(b_last - bc, 0.0))).astype(BF16)
        _advance_state(state_scr, b_last, k_end, vc)


def _gla_kernel(*refs, heads):
    q_refs, k_refs, v_refs, go_refs = (refs[i * heads:(i + 1) * heads] for i in range(4))
    lr_ref, wgk_ref, bgk_ref, gn_ref, o_ref, state_scr, o_scr, b_scr, k_scr, q_scr = refs[4 * heads:]
    dk = q_refs[0].shape[1]
    dv = v_refs[0].shape[1]

    @pl.when(pl.program_id(2) == 0)
    def _():
        state_scr[...] = jnp.zeros_like(state_scr)

    x = _dot(lr_ref[...], wgk_ref[...]) + bgk_ref[...]
    gk = (jnp.minimum(x, 0.0) - jnp.log1p(jnp.exp(-jnp.abs(x)))) * (1.0 / GLA_GATE_NORMALIZER)
    b = _chunk_cumsum(gk, GLA_FAST_CHUNK)
    safe = jnp.max(-b) <= GLA_SAFE_DECAY
    qs = [r[...].astype(F32) * (dk ** -0.5) for r in q_refs]
    ks = [r[...].astype(F32) for r in k_refs]
    head = lambda a, h: a[:, h * dk:(h + 1) * dk]

    @pl.when(safe)
    def _():
        _gla_factorised(qs, ks, [head(b, h) for h in range(heads)], v_refs, state_scr, o_scr)

    @pl.when(jnp.logical_not(safe))
    def _():
        for h in range(heads):
            _gla_nonpositive(qs[h], ks[h], head(gk, h), v_refs[h], state_scr.at[h], o_scr.at[h],
                             b_scr, k_scr, q_scr)

    for h in range(heads):
        o = o_scr[h]
        g = go_refs[h][...].astype(F32)
        o_ref[:, h * dv:(h + 1) * dv] = (o * _rms_scale(o) * gn_ref[...] * (g * _sigmoid(g))).astype(o_ref.dtype)


def gla_mixer(z, z_lr, w_gk, b_gk, g_norm, layer, *, batch, seq, dk, dv, q_col, k_col, v_col, go_col,
              lr_col, tt, heads):
    m = z.shape[0]
    nt = seq // tt
    rows = lambda b, t: b * nt + t
    per_head = lambda width, col0: [
        pl.BlockSpec((tt, width), lambda b, hg, t, h=h: (rows(b, t), col0 + hg * heads + h)) for h in range(heads)]
    return pl.pallas_call(
        functools.partial(_gla_kernel, heads=heads),
        grid=(batch, GLA_HEADS // heads, nt),
        in_specs=per_head(dk, q_col) + per_head(dk, k_col) + per_head(dv, v_col) + per_head(dv, go_col) + [
            pl.BlockSpec((tt, LANES), lambda b, hg, t: (rows(b, t), lr_col)),
            pl.BlockSpec((None, LANES, heads * dk), lambda b, hg, t: (layer, 0, hg)),
            pl.BlockSpec((None, 1, heads * dk), lambda b, hg, t: (layer, 0, hg)),
            pl.BlockSpec((None, 1, dv), lambda b, hg, t: (layer, 0, 0)),
        ],
        out_specs=pl.BlockSpec((tt, heads * dv), lambda b, hg, t: (rows(b, t), hg)),
        out_shape=jax.ShapeDtypeStruct((m, GLA_HEADS * dv), BF16),
        scratch_shapes=[pltpu.VMEM((heads, dk, dv), F32), pltpu.VMEM((heads, tt, dv), F32),
                        pltpu.VMEM((tt, dk), F32), pltpu.VMEM((tt, dk), F32), pltpu.VMEM((tt, dk), F32)],
        compiler_params=_params("parallel", "parallel", "arbitrary"),
        name="gla_mixer",
    )(*([z] * (4 * heads)), z_lr, w_gk, b_gk, g_norm)


def _merge_kernel(a1_ref, a2_ref, a3_ref, m1_ref, m2_ref, m3_ref, w1_ref, w2_ref, w3_ref, o_ref):
    acc = _sigmoid(m1_ref[...].astype(F32)) * _wdot(a1_ref[...], w1_ref)
    acc = acc + _sigmoid(m2_ref[...].astype(F32)) * _wdot(a2_ref[...], w2_ref)
    acc = acc + _sigmoid(m3_ref[...].astype(F32)) * _wdot(a3_ref[...], w3_ref)
    o_ref[...] = acc.astype(o_ref.dtype)


def merge_mixers(a1, a2, a3, z, w1, w2, w3, layer, *, m_col, tm, tn):
    m = a1.shape[0]
    n = w1.shape[-1]
    nb = n // tn
    act = lambda a: pl.BlockSpec((tm, a.shape[1]), lambda i, j: (i, 0))
    gate = lambda c: pl.BlockSpec((tm, tn), lambda i, j: (i, m_col + c * nb + j))
    wgt = lambda w: pl.BlockSpec((None, w.shape[1], tn), lambda i, j: (layer, 0, j))
    return pl.pallas_call(
        _merge_kernel,
        grid=(m // tm, nb),
        in_specs=[act(a1), act(a2), act(a3), gate(0), gate(1), gate(2), wgt(w1), wgt(w2), wgt(w3)],
        out_specs=pl.BlockSpec((tm, tn), lambda i, j: (i, j)),
        out_shape=jax.ShapeDtypeStruct((m, n), BF16),
        compiler_params=_params("parallel", "parallel"),
        name="merge_mixers",
    )(a1, a2, a3, z, z, z, w1, w2, w3)


def _matmul_residual_kernel(a_ref, w_ref, x_ref, o_ref):
    o_ref[...] = x_ref[...] + _wdot(a_ref[...], w_ref)


def matmul_residual(a, w, x, layer, *, tm, tn):
    m, k = a.shape
    n = w.shape[-1]
    return pl.pallas_call(
        _matmul_residual_kernel,
        grid=(m // tm, n // tn),
        in_specs=[pl.BlockSpec((tm, k), lambda i, j: (i, 0)),
                  pl.BlockSpec((None, k, tn), lambda i, j: (layer, 0, j)),
                  pl.BlockSpec((tm, tn), lambda i, j: (i, j))],
        out_specs=pl.BlockSpec((tm, tn), lambda i, j: (i, j)),
        out_shape=jax.ShapeDtypeStruct((m, n), F32),
        compiler_params=_params("parallel", "parallel"),
        name="matmul_residual",
    )(a, w, x)


def _gate_up_kernel(x_ref, g_ref, wg_ref, wu_ref, o_ref, h_scr):
    @pl.when(pl.program_id(1) == 0)
    def _():
        x = x_ref[...]
        h_scr[...] = (x * _rms_scale(x) * g_ref[...]).astype(BF16)

    h = h_scr[...]
    gate = _wdot(h, wg_ref)
    up = _wdot(h, wu_ref)
    o_ref[...] = (gate * _sigmoid(gate) * up).astype(o_ref.dtype)


def gate_up(x, g, w, layer, *, tm, tn):
    m, d = x.shape
    f = w.shape[-1] // 2
    nb = f // tn
    return pl.pallas_call(
        _gate_up_kernel,
        grid=(m // tm, nb),
        in_specs=[pl.BlockSpec((tm, d), lambda i, j: (i, 0)),
                  pl.BlockSpec((None, 1, d), lambda i, j: (layer, 0, 0)),
                  pl.BlockSpec((None, d, tn), lambda i, j: (layer, 0, j)),
                  pl.BlockSpec((None, d, tn), lambda i, j: (layer, 0, nb + j))],
        out_specs=pl.BlockSpec((tm, tn), lambda i, j: (i, j)),
        out_shape=jax.ShapeDtypeStruct((m, f), BF16),
        scratch_shapes=[pltpu.VMEM((tm, d), BF16)],
        compiler_params=_params("parallel", "arbitrary"),
        name="gate_up",
    )(x, g, w, w)


def _ple_kernel(x_ref, g_ref, p_ref, wp_ref, wg_ref, o_ref, h_scr, p_scr):
    tn = o_ref.shape[1]

    @pl.when(pl.program_id(1) == 0)
    def _():
        x = x_ref[...]
        h_scr[...] = (x * _rms_scale(x) * g_ref[...]).astype(BF16)
        p_scr[...] = p_ref[...].astype(BF16)

    j0 = pl.multiple_of(pl.program_id(1) * tn, tn)
    emb = _wdot(p_scr[...], wp_ref)
    gate = _sigmoid(_wdot(h_scr[...], wg_ref))
    o_ref[...] = x_ref[:, pl.ds(j0, tn)] + emb * gate


def ple(x, g, p, wp, wg, layer, *, tm, tn):
    m, d = x.shape
    e = p.shape[-1]
    return pl.pallas_call(
        _ple_kernel,
        grid=(m // tm, d // tn),
        in_specs=[pl.BlockSpec((tm, d), lambda i, j: (i, 0)),
                  pl.BlockSpec((None, 1, d), lambda i, j: (layer, 0, 0)),
                  pl.BlockSpec((None, tm, e), lambda i, j: (layer, i, 0)),
                  pl.BlockSpec((None, e, tn), lambda i, j: (layer, 0, j)),
                  pl.BlockSpec((None, d, tn), lambda i, j: (layer, 0, j))],
        out_specs=pl.BlockSpec((tm, tn), lambda i, j: (i, j)),
        out_shape=jax.ShapeDtypeStruct((m, d), F32),
        scratch_shapes=[pltpu.VMEM((tm, d), BF16), pltpu.VMEM((tm, e), BF16)],
        compiler_params=_params("parallel", "arbitrary"),
        name="ple",
    )(x, g, p, wp, wg)


def _rmsnorm_kernel(x_ref, g_ref, o_ref):
    x = x_ref[...]
    o_ref[...] = x * _rms_scale(x) * g_ref[...]


def rmsnorm(x, g, *, tm):
    m, d = x.shape
    return pl.pallas_call(
        _rmsnorm_kernel,
        grid=(m // tm,),
        in_specs=[pl.BlockSpec((tm, d), lambda i: (i, 0)), pl.BlockSpec((1, d), lambda i: (0, 0))],
        out_specs=pl.BlockSpec((tm, d), lambda i: (i, 0)),
        out_shape=jax.ShapeDtypeStruct((m, d), F32),
        compiler_params=_params("parallel"),
        name="final_rmsnorm",
    )(x, g)


def kernel(x, p, g_mix, w_in, sc_conv_w, sc_conv_b, w_sc_out, cf_conv_w, cf_conv_b, cf_ln_g, cf_ln_b,
           w_cf_out, w_gla_gk, b_gla_gk, g_gla_norm, w_gla_out, w_o, g_ffn, w_gate_up, w_down, g_ple,
           w_ple_gate, w_ple, g_final):
    batch, seq, d = x.shape
    depth = p.shape[0]
    m = batch * seq
    sc_w = sc_conv_w.shape[-1]
    cf_w = cf_conv_w.shape[-1]
    key_dim = w_gla_gk.shape[-1]
    val_dim = w_gla_out.shape[1]
    dk, dv = key_dim // GLA_HEADS, val_dim // GLA_HEADS
    rank = w_gla_gk.shape[1]

    o_sc, o_cf = 0, 3 * sc_w
    o_q = o_cf + 2 * cf_w
    o_k = o_q + key_dim
    o_v = o_k + key_dim
    o_go = o_v + val_dim
    o_lr = o_go + val_dim
    o_m = o_lr + rank

    w_in = w_in.astype(BF16)
    w_gate = w_in[:, :, o_m:]
    w_lr = jnp.pad(w_in[:, :, o_lr:o_m], ((0, 0), (0, 0), (0, LANES - rank)))
    w_sc_out, w_cf_out, w_gla_out, w_o = (a.astype(BF16) for a in (w_sc_out, w_cf_out, w_gla_out, w_o))
    w_down, w_ple_gate, w_ple = (a.astype(BF16) for a in (w_down, w_ple_gate, w_ple))
    w_gk = jnp.pad(w_gla_gk, ((0, 0), (0, LANES - rank), (0, 0))).astype(BF16)
    vec = lambda a: a.reshape(a.shape[0], 1, a.shape[1])
    g_mix, g_ffn, g_ple = vec(g_mix), vec(g_ffn), vec(g_ple)
    sc_conv_b, cf_conv_b, cf_ln_g, cf_ln_b = vec(sc_conv_b), vec(cf_conv_b), vec(cf_ln_g), vec(cf_ln_b)
    b_gla_gk, g_gla_norm = vec(b_gla_gk), vec(g_gla_norm)

    tm_big = min(1024, m)
    x = x.reshape(m, d)
    p = p.reshape(depth, m, p.shape[-1])
    for i in range(depth):
        z = norm_matmul(x, g_mix, w_in, i, n=o_lr, tm=tm_big, tn=_tile(o_lr, 1024))
        z_gate, z_lr, a_sc, a_cf = gate_proj_and_convs(
            x, g_mix, w_gate, w_lr, z, sc_conv_w, sc_conv_b, cf_conv_w, cf_conv_b, cf_ln_g, cf_ln_b, i,
            seq=seq, sc_width=sc_w, sc_col=o_sc // sc_w, cf_width=cf_w, cf_col=o_cf // cf_w,
            tm=tm_big, tn=_tile(3 * d, 768))
        a_gla = gla_mixer(z, z_lr, w_gk, b_gla_gk, g_gla_norm, i, batch=batch, seq=seq, dk=dk, dv=dv,
                          q_col=o_q // dk, k_col=o_k // dk, v_col=o_v // dv, go_col=o_go // dv,
                          lr_col=0, tt=min(512, seq), heads=GLA_HEADS)
        merged = merge_mixers(a_sc, a_cf, a_gla, z_gate, w_sc_out, w_cf_out, w_gla_out, i,
                              m_col=0, tm=tm_big, tn=_tile(d, 512))
        x = matmul_residual(merged, w_o, x, i, tm=tm_big, tn=_tile(d, 1024))
        act = gate_up(x, g_ffn, w_gate_up, i, tm=tm_big, tn=_tile(w_down.shape[1], 512))
        x = matmul_residual(act, w_down, x, i, tm=tm_big, tn=_tile(d, 512))
        x = ple(x, g_ple, p, w_ple, w_ple_gate, i, tm=tm_big, tn=_tile(d, 1024))
    out = rmsnorm(x, g_final.reshape(1, d), tm=min(512, m))
    return out.reshape(batch, seq, d)
```

```python
import functools

import jax
import jax.numpy as jnp
from jax import lax
from jax.experimental import pallas as pl
from jax.experimental.pallas import tpu as pltpu

F32 = jnp.float32
BF16 = jnp.bfloat16

EPS = 1e-6
SC_KERNEL = 3
CF_KERNEL = 31
GLA_HEADS = 4
GLA_GATE_NORMALIZER = 16.0
GLA_FAST_CHUNK = 256
GLA_SAFE_DECAY = 60.0
GLA_CHUNK = 64
GLA_SUB = 16
LANES = 128
F32_SUBLANES = 8
BF16_SUBLANES = 16
CONV_ROWS = 8
VMEM_LIMIT_BYTES = 56 * 1024 * 1024

ROW_TILE = 1024
PROJ_COLS = 1024
GATE_COLS = 768
MERGE_COLS = 512
OUT_COLS = 1024
FFN_COLS = 512
PLE_COLS = 1024
GLA_ROWS = 512
NORM_ROWS = 512


def _tile(n, want):
    t = min(want, n) // LANES * LANES
    while n % t:
        t -= LANES
    return t


def _params(*sem):
    return pltpu.CompilerParams(dimension_semantics=sem, vmem_limit_bytes=VMEM_LIMIT_BYTES)


def _rms_scale(x):
    return lax.rsqrt(jnp.mean(x * x, axis=-1, keepdims=True) + EPS)


def _sigmoid(x):
    return 1.0 / (1.0 + jnp.exp(-x))


def _dot(a, b):
    return jnp.dot(a, b, preferred_element_type=F32)


def _wdot(a, w_ref):
    return _dot(a, w_ref[...].astype(BF16))


def _norm_matmul_kernel(x_ref, g_ref, w_ref, o_ref, h_scr):
    @pl.when(pl.program_id(1) == 0)
    def _():
        x = x_ref[...]
        h_scr[...] = (x * _rms_scale(x) * g_ref[...]).astype(BF16)

    o_ref[...] = _wdot(h_scr[...], w_ref).astype(o_ref.dtype)


def norm_matmul(x, g, w, layer, *, n, tm, tn):
    m, d = x.shape
    return pl.pallas_call(
        _norm_matmul_kernel,
        grid=(m // tm, n // tn),
        in_specs=[
            pl.BlockSpec((tm, d), lambda i, j: (i, 0)),
            pl.BlockSpec((None, 1, d), lambda i, j: (layer, 0, 0)),
            pl.BlockSpec((None, d, tn), lambda i, j: (layer, 0, j)),
        ],
        out_specs=pl.BlockSpec((tm, tn), lambda i, j: (i, j)),
        out_shape=jax.ShapeDtypeStruct((m, n), BF16),
        scratch_shapes=[pltpu.VMEM((tm, d), BF16)],
        compiler_params=_params("parallel", "arbitrary"),
        name="norm_matmul",
    )(x, g, w)


def _sc_rows(first, b_ref, c_ref, x_ref, ch_ref, xh_ref, w_ref, bias_ref, o_ref, y_scr):
    halo = ch_ref.shape[0]
    tm = c_ref.shape[0]
    yh = ch_ref[...].astype(F32) * xh_ref[...].astype(F32)
    y_scr[0:halo, :] = jnp.where(first, 0.0, yh)
    y = c_ref[...].astype(F32) * x_ref[...].astype(F32)
    y_scr[halo:halo + tm, :] = y
    u = bias_ref[...] + w_ref[SC_KERNEL - 1:SC_KERNEL, :] * y
    for k in range(SC_KERNEL - 1):
        off = halo - (SC_KERNEL - 1) + k
        u = u + w_ref[k:k + 1, :] * y_scr[off:off + tm, :]
    o_ref[...] = (b_ref[...].astype(F32) * u).astype(o_ref.dtype)


def _cf_prepare(first, a_ref, g_ref, ah_ref, gh_ref, y_scr):
    halo = ah_ref.shape[0]
    tm = a_ref.shape[0]
    yh = ah_ref[...].astype(F32) * _sigmoid(gh_ref[...].astype(F32))
    y_scr[0, 0:halo, :] = jnp.where(first, 0.0, yh)
    y_scr[0, halo:halo + tm, :] = a_ref[...].astype(F32) * _sigmoid(g_ref[...].astype(F32))
    span = tm + halo - F32_SUBLANES
    for s in range(1, F32_SUBLANES):
        y_scr[s, 0:span, :] = y_scr[0, s:s + span, :]


def _cf_conv_rows(w_ref, bias_ref, lng_ref, lnb_ref, o_ref, y_scr):
    halo = y_scr.shape[1] - o_ref.shape[0]
    rows = w_ref.shape[1]
    carry = jnp.zeros((rows, bias_ref.shape[1]), F32)
    for r0 in range(0, o_ref.shape[0], rows):
        acc = bias_ref[...] + carry
        for k in range(CF_KERNEL):
            off = halo - (CF_KERNEL - 1) + k + r0
            s, base = off % F32_SUBLANES, off // F32_SUBLANES * F32_SUBLANES
            acc = acc + w_ref[k] * y_scr[s, base:base + rows, :]
        mu = jnp.mean(acc, axis=-1, keepdims=True)
        cen = acc - mu
        var = jnp.mean(cen * cen, axis=-1, keepdims=True)
        u = cen * lax.rsqrt(var + EPS) * lng_ref[...] + lnb_ref[...]
        o_ref[r0:r0 + rows, :] = (u * _sigmoid(u)).astype(o_ref.dtype)
        carry = 0.0 * acc


def _gate_conv_kernel(x_ref, g_ref, w_ref, wlr_ref,
                      scb_ref, scc_ref, scx_ref, scch_ref, scxh_ref, scw_ref, scbias_ref,
                      cfa_ref, cfg_ref, cfah_ref, cfgh_ref, cfw_ref, cfbias_ref, lng_ref, lnb_ref,
                      o_ref, lr_ref, asc_ref, acf_ref, h_scr, sc_scr, cf_scr, *, slices_per_seq):
    i, j = pl.program_id(0), pl.program_id(1)

    @pl.when(j == 0)
    def _():
        x = x_ref[...]
        h_scr[...] = (x * _rms_scale(x) * g_ref[...]).astype(BF16)
        lr_ref[...] = _dot(h_scr[...], wlr_ref[...]).astype(lr_ref.dtype)

    o_ref[...] = _dot(h_scr[...], w_ref[...]).astype(o_ref.dtype)
    first = (i * pl.num_programs(1) + j) % slices_per_seq == 0
    _sc_rows(first, scb_ref, scc_ref, scx_ref, scch_ref, scxh_ref, scw_ref, scbias_ref, asc_ref, sc_scr)
    _cf_prepare(first, cfa_ref, cfg_ref, cfah_ref, cfgh_ref, cf_scr)
    _cf_conv_rows(cfw_ref, cfbias_ref, lng_ref, lnb_ref, acf_ref, cf_scr)


def gate_proj_and_convs(x, g, w_gate, w_lr, z, sc_w, sc_b, cf_w, cf_b, ln_g, ln_b, layer,
                        *, seq, sc_width, sc_col, cf_width, cf_col, tm, tn):
    m, d = x.shape
    n = w_gate.shape[-1]
    nb = n // tn
    rs = tm // nb
    sc_halo, cf_halo = BF16_SUBLANES, 2 * BF16_SUBLANES
    assert tm % nb == 0 and rs % cf_halo == 0 and seq % rs == 0 and cf_halo >= CF_KERNEL - 1
    rows = lambda wd, col: pl.BlockSpec((rs, wd), lambda i, j: (i * nb + j, col))
    prev = lambda halo, wd, col: pl.BlockSpec(
        (halo, wd), lambda i, j: (jnp.maximum((i * nb + j) * (rs // halo) - 1, 0), col))
    vec = lambda wd: pl.BlockSpec((None, 1, wd), lambda i, j: (layer, 0, 0))
    cf_w_rows = jnp.broadcast_to(cf_w[:, :, None, :], cf_w.shape[:2] + (CONV_ROWS, cf_width))
    return pl.pallas_call(
        functools.partial(_gate_conv_kernel, slices_per_seq=seq // rs),
        grid=(m // tm, nb),
        in_specs=[
            pl.BlockSpec((tm, d), lambda i, j: (i, 0)),
            pl.BlockSpec((None, 1, d), lambda i, j: (layer, 0, 0)),
            pl.BlockSpec((None, d, tn), lambda i, j: (layer, 0, j)),
            pl.BlockSpec((None, d, LANES), lambda i, j: (layer, 0, 0)),
            rows(sc_width, sc_col), rows(sc_width, sc_col + 1), rows(sc_width, sc_col + 2),
            prev(sc_halo, sc_width, sc_col + 1), prev(sc_halo, sc_width, sc_col + 2),
            pl.BlockSpec((None, SC_KERNEL, sc_width), lambda i, j: (layer, 0, 0)), vec(sc_width),
            rows(cf_width, cf_col), rows(cf_width, cf_col + 1),
            prev(cf_halo, cf_width, cf_col), prev(cf_halo, cf_width, cf_col + 1),
            pl.BlockSpec((None, CF_KERNEL, CONV_ROWS, cf_width), lambda i, j: (layer, 0, 0, 0)),
            vec(cf_width), vec(cf_width), vec(cf_width),
        ],
        out_specs=[
            pl.BlockSpec((tm, tn), lambda i, j: (i, j)),
            pl.BlockSpec((tm, LANES), lambda i, j: (i, 0)),
            pl.BlockSpec((rs, sc_width), lambda i, j: (i * nb + j, 0)),
            pl.BlockSpec((rs, cf_width), lambda i, j: (i * nb + j, 0)),
        ],
        out_shape=[jax.ShapeDtypeStruct((m, n), BF16), jax.ShapeDtypeStruct((m, LANES), BF16),
                   jax.ShapeDtypeStruct((m, sc_width), BF16), jax.ShapeDtypeStruct((m, cf_width), BF16)],
        scratch_shapes=[pltpu.VMEM((tm, d), BF16), pltpu.VMEM((rs + sc_halo, sc_width), F32),
                        pltpu.VMEM((F32_SUBLANES, rs + cf_halo, cf_width), F32)],
        compiler_params=_params("parallel", "arbitrary"),
        name="gate_proj_and_convs",
    )(x, g, w_gate, w_lr, z, z, z, z, z, sc_w, sc_b, z, z, z, z, cf_w_rows, cf_b, ln_g, ln_b)


def _chunk_cumsum(gk, chunk):
    row = lax.broadcasted_iota(jnp.int32, (chunk, chunk), 0)
    col = lax.broadcasted_iota(jnp.int32, (chunk, chunk), 1)
    tril = jnp.where(col <= row, 1.0, 0.0).astype(BF16)
    g_hi = gk.astype(BF16)
    g_lo = (gk - g_hi.astype(F32)).astype(BF16)
    sums = [_dot(tril, g_hi[r0:r0 + chunk]) + _dot(tril, g_lo[r0:r0 + chunk])
            for r0 in range(0, gk.shape[0], chunk)]
    return jnp.concatenate(sums, axis=0)


def _advance_state(state_scr, b_last, k_end, vc):
    dk, dv = state_scr.shape
    upd = lax.dot_general(k_end, vc, (((0,), (0,)), ((), ())), preferred_element_type=F32)
    dec_col = jnp.transpose(jnp.broadcast_to(jnp.exp(b_last), (LANES, dk)))
    for l0 in range(0, dv, LANES):
        state_scr[:, l0:l0 + LANES] = state_scr[:, l0:l0 + LANES] * dec_col + upd[:, l0:l0 + LANES]


def _gla_factorised(qs, ks, bs, v_refs, state_scr, o_scr):
    tt = qs[0].shape[0]
    row = lax.broadcasted_iota(jnp.int32, (GLA_FAST_CHUNK, GLA_FAST_CHUNK), 0)
    col = lax.broadcasted_iota(jnp.int32, (GLA_FAST_CHUNK, GLA_FAST_CHUNK), 1)
    for base in range(0, tt, GLA_FAST_CHUNK):
        rows = slice(base, base + GLA_FAST_CHUNK)
        for h, (q, k, b, v_ref) in enumerate(zip(qs, ks, bs, v_refs)):
            bc, kc, vc = b[rows], k[rows], v_ref[rows, :]
            q_dec = (q[rows] * jnp.exp(bc)).astype(BF16)
            k_inv = (kc * jnp.exp(-bc)).astype(BF16)
            scores = lax.dot_general(q_dec, k_inv, (((1,), (1,)), ((), ())), preferred_element_type=F32)
            scores = jnp.where(col <= row, scores, 0.0).astype(BF16)
            o_scr[h, rows, :] = _dot(q_dec, state_scr[h].astype(BF16)) + _dot(scores, vc)
            b_last = bc[GLA_FAST_CHUNK - 1:GLA_FAST_CHUNK]
            k_end = (kc * jnp.exp(b_last - bc)).astype(BF16)
            _advance_state(state_scr.at[h], b_last, k_end, vc)


def _gla_nonpositive(q, k, gk, v_ref, state_scr, o_scr, b_scr, k_scr, q_scr):
    tt, dk = q.shape
    n_sub = GLA_CHUNK // GLA_SUB
    b_scr[...] = _chunk_cumsum(gk, GLA_CHUNK)
    k_scr[...] = k
    q_scr[...] = q
    rows_sub = lax.broadcasted_iota(jnp.int32, (GLA_SUB, dk), 0)
    rows_chunk = lax.broadcasted_iota(jnp.int32, (GLA_CHUNK, dk), 0)
    lane_chunk = lax.broadcasted_iota(jnp.int32, (GLA_SUB, GLA_CHUNK), 1)

    for base in range(0, tt, GLA_CHUNK):
        bc = b_scr[base:base + GLA_CHUNK, :]
        kc = k_scr[base:base + GLA_CHUNK, :]
        qc = q_scr[base:base + GLA_CHUNK, :]
        vc = v_ref[base:base + GLA_CHUNK, :]

        score_rows = []
        for blk in range(n_sub):
            r0 = base + blk * GLA_SUB
            q_blk = q_scr[r0:r0 + GLA_SUB, :]
            b_blk = b_scr[r0:r0 + GLA_SUB, :]

            def diag_step(j, s_acc, r0=r0, q_blk=q_blk, b_blk=b_blk, blk=blk):
                bj = b_scr[pl.ds(r0 + j, 1), :]
                kj = k_scr[pl.ds(r0 + j, 1), :]
                decay = jnp.where(rows_sub >= j, jnp.exp(jnp.minimum(b_blk - bj, 0.0)), 0.0)
                s = jnp.sum(q_blk * decay * kj, axis=-1, keepdims=True)
                return jnp.where(lane_chunk == blk * GLA_SUB + j, s, s_acc)

            s_blk = lax.fori_loop(0, GLA_SUB, diag_step, jnp.zeros((GLA_SUB, GLA_CHUNK), F32))
            if blk > 0:
                b_start = b_scr[r0 - 1:r0, :]
                k_dec = jnp.where(rows_chunk < blk * GLA_SUB,
                                  jnp.exp(jnp.minimum(b_start - bc, 0.0)), 0.0)
                k_sc = (kc * k_dec).astype(BF16)
                q_sc = (q_blk * jnp.exp(jnp.minimum(b_blk - b_start, 0.0))).astype(BF16)
                s_blk = s_blk + lax.dot_general(q_sc, k_sc, (((1,), (1,)), ((), ())),
                                                preferred_element_type=F32)
            score_rows.append(s_blk)
        scores = jnp.concatenate(score_rows, axis=0).astype(BF16)

        q_dec = (qc * jnp.exp(bc)).astype(BF16)
        o_scr[base:base + GLA_CHUNK, :] = _dot(q_dec, state_scr[...].astype(BF16)) + _dot(scores, vc)
        b_last = b_scr[base + GLA_CHUNK - 1:base + GLA_CHUNK, :]
        k_end = (kc * jnp.exp(jnp.minimum(b_last - bc, 0.0))).astype(BF16)
        _advance_state(state_scr, b_last, k_end, vc)


def _gla_kernel(*refs, heads):
    q_refs, k_refs, v_refs, go_refs = (refs[i * heads:(i + 1) * heads] for i in range(4))
    lr_ref, wgk_ref, bgk_ref, gn_ref, o_ref, state_scr, o_scr, b_scr, k_scr, q_scr = refs[4 * heads:]
    dk = q_refs[0].shape[1]
    dv = v_refs[0].shape[1]

    @pl.when(pl.program_id(2) == 0)
    def _():
        state_scr[...] = jnp.zeros_like(state_scr)

    x = _dot(lr_ref[...], wgk_ref[...]) + bgk_ref[...]
    gk = (jnp.minimum(x, 0.0) - jnp.log(1.0 + jnp.exp(-jnp.abs(x)))) * (1.0 / GLA_GATE_NORMALIZER)
    b = _chunk_cumsum(gk, GLA_FAST_CHUNK)
    safe = jnp.min(b) >= -GLA_SAFE_DECAY
    qs = [r[...].astype(F32) * (dk ** -0.5) for r in q_refs]
    ks = [r[...].astype(F32) for r in k_refs]
    head = lambda a, h: a[:, h * dk:(h + 1) * dk]

    @pl.when(safe)
    def _():
        _gla_factorised(qs, ks, [head(b, h) for h in range(heads)], v_refs, state_scr, o_scr)

    @pl.when(jnp.logical_not(safe))
    def _():
        for h in range(heads):
            _gla_nonpositive(qs[h], ks[h], head(gk, h), v_refs[h], state_scr.at[h], o_scr.at[h],
                             b_scr, k_scr, q_scr)

    for h in range(heads):
        o = o_scr[h]
        g = go_refs[h][...].astype(F32)
        o_ref[:, h * dv:(h + 1) * dv] = (o * _rms_scale(o) * gn_ref[...] * (g * _sigmoid(g))).astype(o_ref.dtype)


def gla_mixer(z, z_lr, w_gk, b_gk, g_norm, layer, *, batch, seq, dk, dv, q_col, k_col, v_col, go_col,
              lr_col, tt, heads):
    m = z.shape[0]
    nt = seq // tt
    rows = lambda b, t: b * nt + t
    per_head = lambda width, col0: [
        pl.BlockSpec((tt, width), lambda b, hg, t, h=h: (rows(b, t), col0 + hg * heads + h)) for h in range(heads)]
    return pl.pallas_call(
        functools.partial(_gla_kernel, heads=heads),
        grid=(batch, GLA_HEADS // heads, nt),
        in_specs=per_head(dk, q_col) + per_head(dk, k_col) + per_head(dv, v_col) + per_head(dv, go_col) + [
            pl.BlockSpec((tt, LANES), lambda b, hg, t: (rows(b, t), lr_col)),
            pl.BlockSpec((None, LANES, heads * dk), lambda b, hg, t: (layer, 0, hg)),
            pl.BlockSpec((None, 1, heads * dk), lambda b, hg, t: (layer, 0, hg)),
            pl.BlockSpec((None, 1, dv), lambda b, hg, t: (layer, 0, 0)),
        ],
        out_specs=pl.BlockSpec((tt, heads * dv), lambda b, hg, t: (rows(b, t), hg)),
        out_shape=jax.ShapeDtypeStruct((m, GLA_HEADS * dv), BF16),
        scratch_shapes=[pltpu.VMEM((heads, dk, dv), F32), pltpu.VMEM((heads, tt, dv), F32),
                        pltpu.VMEM((tt, dk), F32), pltpu.VMEM((tt, dk), F32), pltpu.VMEM((tt, dk), F32)],
        compiler_params=_params("parallel", "parallel", "arbitrary"),
        name="gla_mixer",
    )(*([z] * (4 * heads)), z_lr, w_gk, b_gk, g_norm)


def _merge_kernel(a1_ref, a2_ref, a3_ref, m1_ref, m2_ref, m3_ref, w1_ref, w2_ref, w3_ref, o_ref):
    acc = _sigmoid(m1_ref[...].astype(F32)) * _wdot(a1_ref[...], w1_ref)
    acc = acc + _sigmoid(m2_ref[...].astype(F32)) * _wdot(a2_ref[...], w2_ref)
    acc = acc + _sigmoid(m3_ref[...].astype(F32)) * _wdot(a3_ref[...], w3_ref)
    o_ref[...] = acc.astype(o_ref.dtype)


def merge_mixers(a1, a2, a3, z, w1, w2, w3, layer, *, m_col, tm, tn):
    m = a1.shape[0]
    n = w1.shape[-1]
    nb = n // tn
    act = lambda a: pl.BlockSpec((tm, a.shape[1]), lambda i, j: (i, 0))
    gate = lambda c: pl.BlockSpec((tm, tn), lambda i, j: (i, m_col + c * nb + j))
    wgt = lambda w: pl.BlockSpec((None, w.shape[1], tn), lambda i, j: (layer, 0, j))
    return pl.pallas_call(
        _merge_kernel,
        grid=(m // tm, nb),
        in_specs=[act(a1), act(a2), act(a3), gate(0), gate(1), gate(2), wgt(w1), wgt(w2), wgt(w3)],
        out_specs=pl.BlockSpec((tm, tn), lambda i, j: (i, j)),
        out_shape=jax.ShapeDtypeStruct((m, n), BF16),
        compiler_params=_params("parallel", "parallel"),
        name="merge_mixers",
    )(a1, a2, a3, z, z, z, w1, w2, w3)


def _matmul_residual_kernel(a_ref, w_ref, x_ref, o_ref):
    o_ref[...] = x_ref[...] + _wdot(a_ref[...], w_ref)


def matmul_residual(a, w, x, layer, *, tm, tn):
    m, k = a.shape
    n = w.shape[-1]
    return pl.pallas_call(
        _matmul_residual_kernel,
        grid=(m // tm, n // tn),
        in_specs=[pl.BlockSpec((tm, k), lambda i, j: (i, 0)),
                  pl.BlockSpec((None, k, tn), lambda i, j: (layer, 0, j)),
                  pl.BlockSpec((tm, tn), lambda i, j: (i, j))],
        out_specs=pl.BlockSpec((tm, tn), lambda i, j: (i, j)),
        out_shape=jax.ShapeDtypeStruct((m, n), F32),
        compiler_params=_params("parallel", "parallel"),
        name="matmul_residual",
    )(a, w, x)


def _gate_up_kernel(x_ref, g_ref, wg_ref, wu_ref, o_ref, h_scr):
    @pl.when(pl.program_id(1) == 0)
    def _():
        x = x_ref[...]
        h_scr[...] = (x * _rms_scale(x) * g_ref[...]).astype(BF16)

    h = h_scr[...]
    gate = _wdot(h, wg_ref)
    up = _wdot(h, wu_ref)
    o_ref[...] = (gate * _sigmoid(gate) * up).astype(o_ref.dtype)


def gate_up(x, g, w, layer, *, tm, tn):
    m, d = x.shape
    f = w.shape[-1] // 2
    nb = f // tn
    return pl.pallas_call(
        _gate_up_kernel,
        grid=(m // tm, nb),
        in_specs=[pl.BlockSpec((tm, d), lambda i, j: (i, 0)),
                  pl.BlockSpec((None, 1, d), lambda i, j: (layer, 0, 0)),
                  pl.BlockSpec((None, d, tn), lambda i, j: (layer, 0, j)),
                  pl.BlockSpec((None, d, tn), lambda i, j: (layer, 0, nb + j))],
        out_specs=pl.BlockSpec((tm, tn), lambda i, j: (i, j)),
        out_shape=jax.ShapeDtypeStruct((m, f), BF16),
        scratch_shapes=[pltpu.VMEM((tm, d), BF16)],
        compiler_params=_params("parallel", "arbitrary"),
        name="gate_up",
    )(x, g, w, w)


def _ple_kernel(x_ref, g_ref, p_ref, wp_ref, wg_ref, o_ref, h_scr, p_scr):
    tn = o_ref.shape[1]

    @pl.when(pl.program_id(1) == 0)
    def _():
        x = x_ref[...]
        h_scr[...] = (x * _rms_scale(x) * g_ref[...]).astype(BF16)
        p_scr[...] = p_ref[...].astype(BF16)

    j0 = pl.multiple_of(pl.program_id(1) * tn, tn)
    emb = _wdot(p_scr[...], wp_ref)
    gate = _sigmoid(_wdot(h_scr[...], wg_ref))
    o_ref[...] = x_ref[:, pl.ds(j0, tn)] + emb * gate


def ple(x, g, p, wp, wg, layer, *, tm, tn):
    m, d = x.shape
    e = p.shape[-1]
    return pl.pallas_call(
        _ple_kernel,
        grid=(m // tm, d // tn),
        in_specs=[pl.BlockSpec((tm, d), lambda i, j: (i, 0)),
                  pl.BlockSpec((None, 1, d), lambda i, j: (layer, 0, 0)),
                  pl.BlockSpec((None, tm, e), lambda i, j: (layer, i, 0)),
                  pl.BlockSpec((None, e, tn), lambda i, j: (layer, 0, j)),
                  pl.BlockSpec((None, d, tn), lambda i, j: (layer, 0, j))],
        out_specs=pl.BlockSpec((tm, tn), lambda i, j: (i, j)),
        out_shape=jax.ShapeDtypeStruct((m, d), F32),
        scratch_shapes=[pltpu.VMEM((tm, d), BF16), pltpu.VMEM((tm, e), BF16)],
        compiler_params=_params("parallel", "arbitrary"),
        name="ple",
    )(x, g, p, wp, wg)


def _rmsnorm_kernel(x_ref, g_ref, o_ref):
    x = x_ref[...]
    o_ref[...] = x * _rms_scale(x) * g_ref[...]


def rmsnorm(x, g, *, tm):
    m, d = x.shape
    return pl.pallas_call(
        _rmsnorm_kernel,
        grid=(m // tm,),
        in_specs=[pl.BlockSpec((tm, d), lambda i: (i, 0)), pl.BlockSpec((1, d), lambda i: (0, 0))],
        out_specs=pl.BlockSpec((tm, d), lambda i: (i, 0)),
        out_shape=jax.ShapeDtypeStruct((m, d), F32),
        compiler_params=_params("parallel"),
        name="final_rmsnorm",
    )(x, g)


def kernel(x, p, g_mix, w_in, sc_conv_w, sc_conv_b, w_sc_out, cf_conv_w, cf_conv_b, cf_ln_g, cf_ln_b,
           w_cf_out, w_gla_gk, b_gla_gk, g_gla_norm, w_gla_out, w_o, g_ffn, w_gate_up, w_down, g_ple,
           w_ple_gate, w_ple, g_final):
    batch, seq, d = x.shape
    depth = p.shape[0]
    m = batch * seq
    sc_w = sc_conv_w.shape[-1]
    cf_w = cf_conv_w.shape[-1]
    key_dim = w_gla_gk.shape[-1]
    val_dim = w_gla_out.shape[1]
    dk, dv = key_dim // GLA_HEADS, val_dim // GLA_HEADS
    rank = w_gla_gk.shape[1]

    o_sc, o_cf = 0, 3 * sc_w
    o_q = o_cf + 2 * cf_w
    o_k = o_q + key_dim
    o_v = o_k + key_dim
    o_go = o_v + val_dim
    o_lr = o_go + val_dim
    o_m = o_lr + rank

    w_in = w_in.astype(BF16)
    w_gate = w_in[:, :, o_m:]
    w_lr = jnp.pad(w_in[:, :, o_lr:o_m], ((0, 0), (0, 0), (0, LANES - rank)))
    w_sc_out, w_cf_out, w_gla_out, w_o = (a.astype(BF16) for a in (w_sc_out, w_cf_out, w_gla_out, w_o))
    w_down, w_ple_gate, w_ple = (a.astype(BF16) for a in (w_down, w_ple_gate, w_ple))
    w_gk = jnp.pad(w_gla_gk, ((0, 0), (0, LANES - rank), (0, 0))).astype(BF16)
    vec = lambda a: a.reshape(a.shape[0], 1, a.shape[1])
    g_mix, g_ffn, g_ple = vec(g_mix), vec(g_ffn), vec(g_ple)
    sc_conv_b, cf_conv_b, cf_ln_g, cf_ln_b = vec(sc_conv_b), vec(cf_conv_b), vec(cf_ln_g), vec(cf_ln_b)
    b_gla_gk, g_gla_norm = vec(b_gla_gk), vec(g_gla_norm)

    tm_big = min(ROW_TILE, m)
    x = x.reshape(m, d)
    p = p.reshape(depth, m, p.shape[-1])
    for i in range(depth):
        z = norm_matmul(x, g_mix, w_in, i, n=o_lr, tm=tm_big, tn=_tile(o_lr, PROJ_COLS))
        z_gate, z_lr, a_sc, a_cf = gate_proj_and_convs(
            x, g_mix, w_gate, w_lr, z, sc_conv_w, sc_conv_b, cf_conv_w, cf_conv_b, cf_ln_g, cf_ln_b, i,
            seq=seq, sc_width=sc_w, sc_col=o_sc // sc_w, cf_width=cf_w, cf_col=o_cf // cf_w,
            tm=tm_big, tn=_tile(3 * d, GATE_COLS))
        a_gla = gla_mixer(z, z_lr, w_gk, b_gla_gk, g_gla_norm, i, batch=batch, seq=seq, dk=dk, dv=dv,
                          q_col=o_q // dk, k_col=o_k // dk, v_col=o_v // dv, go_col=o_go // dv,
                          lr_col=0, tt=min(GLA_ROWS, seq), heads=GLA_HEADS)
        merged = merge_mixers(a_sc, a_cf, a_gla, z_gate, w_sc_out, w_cf_out, w_gla_out, i,
                              m_col=0, tm=tm_big, tn=_tile(d, MERGE_COLS))
        x = matmul_residual(merged, w_o, x, i, tm=tm_big, tn=_tile(d, OUT_COLS))
        act = gate_up(x, g_ffn, w_gate_up, i, tm=tm_big, tn=_tile(w_down.shape[1], FFN_COLS))
        x = matmul_residual(act, w_down, x, i, tm=tm_big, tn=_tile(d, FFN_COLS))
        x = ple(x, g_ple, p, w_ple, w_ple_gate, i, tm=tm_big, tn=_tile(d, PLE_COLS))
    out = rmsnorm(x, g_final.reshape(1, d), tm=min(NORM_ROWS, m))
    return out.reshape(batch, seq, d)
```
